```python
import math
import jax
import jax.numpy as jnp
from jax import lax
import numpy as np

D_MODEL = 1024
BATCH = 2
SEQ = 8192
DEPTH = 2
DEC_BATCH = 128
DEC_SEQ = 4
PAST_LEN = 2048
PAGE_SIZE = 128

SB_HEAD_DIM = 64
SB_HEADS = (D_MODEL // 2) // SB_HEAD_DIM
SB_WIDTH = SB_HEADS * SB_HEAD_DIM
SB_QBLOCK = 128
SB_BIAS_MIN = 4.0
SB_BIAS_MAX = 8.0
POOL_WINDOWS = (2, 4, 8, 16)
POOL_GROUPS = 4
POOL_WIDTH = D_MODEL // 2
POOL_GROUP_DIM = POOL_WIDTH // POOL_GROUPS
POOL_STATE = max(POOL_WINDOWS) - 1
GLA_HEADS = 4
GLA_KEY_WIDTH = D_MODEL // 2
GLA_VAL_WIDTH = D_MODEL
GLA_DK = GLA_KEY_WIDTH // GLA_HEADS
GLA_DV = GLA_VAL_WIDTH // GLA_HEADS
GLA_RANK = 16
GLA_TAU = 16.0
GLA_CHUNK = 64
N_BRANCH = 3
EPS = 1e-6
N_IN = 4 * SB_WIDTH + 2 * POOL_WIDTH + 2 * GLA_KEY_WIDTH + 2 * GLA_VAL_WIDTH + GLA_RANK + N_BRANCH * D_MODEL

kernel_name = "hybrid_stickbreak_pool_gla_step"


def _split_points():
    sizes = (SB_WIDTH, SB_WIDTH, SB_WIDTH, SB_WIDTH,
             POOL_WIDTH, POOL_WIDTH,
             GLA_KEY_WIDTH, GLA_KEY_WIDTH, GLA_VAL_WIDTH,
             GLA_VAL_WIDTH, GLA_RANK,
             N_BRANCH * D_MODEL)
    pts, acc = [], 0
    for s in sizes[:-1]:
        acc += s
        pts.append(acc)
    return pts


def rms_norm(x, g):
    xf = x.astype(jnp.float32)
    y = xf * lax.rsqrt(jnp.mean(xf * xf, axis=-1, keepdims=True) + EPS)
    return (y * g.astype(jnp.float32)).astype(x.dtype)


def stick_breaking_attention(q, k, v, q_pos, k_pos, bias):
    B, Tq, H, Dh = q.shape
    blk = SB_QBLOCK if Tq % SB_QBLOCK == 0 else Tq
    nb = Tq // blk
    qb = q.astype(jnp.float32).reshape(B, nb, blk, H, Dh).transpose(1, 0, 2, 3, 4)
    pb = q_pos.reshape(nb, blk)
    kf = k.astype(jnp.float32)
    vf = v.astype(jnp.float32)
    bf = bias.astype(jnp.float32)[None, :, None, None]
    scale = 1.0 / math.sqrt(Dh)

    def one_block(args):
        qi, pi = args
        z = jnp.einsum("bqhd,bkhd->bhqk", qi, kf) * scale + bf
        mask = (k_pos[None, :] < pi[:, None])[None, None]
        log_1mb = jnp.where(mask, jax.nn.log_sigmoid(-z), 0.0)
        suffix = lax.cumsum(log_1mb, axis=3, reverse=True) - log_1mb
        a = jnp.where(mask, jnp.exp(jax.nn.log_sigmoid(z) + suffix), 0.0)
        return jnp.einsum("bhqk,bkhd->bqhd", a, vf)

    o = lax.map(one_block, (qb, pb))
    return o.transpose(1, 0, 2, 3, 4).reshape(B, Tq, H, Dh).astype(q.dtype)


def multiscale_pool(u_ext, start_pos, pool_w, pool_scale):
    B, L, W = u_ext.shape
    T = L - POOL_STATE
    uf = u_ext.astype(jnp.float32)
    cs = jnp.concatenate([jnp.zeros((B, 1, W), jnp.float32), jnp.cumsum(uf, axis=1)], axis=1)
    pos = start_pos + jnp.arange(T, dtype=jnp.int32)
    cur = uf[:, POOL_STATE:]
    groups = []
    for g, w in enumerate(POOL_WINDOWS):
        c = slice(g * POOL_GROUP_DIM, (g + 1) * POOL_GROUP_DIM)
        win_sum = cs[:, POOL_STATE + 1:POOL_STATE + 1 + T, c] - cs[:, POOL_STATE + 1 - w:POOL_STATE + 1 - w + T, c]
        count = jnp.minimum(w, pos + 1).astype(jnp.float32)[None, :, None]
        groups.append(win_sum / count - cur[..., c])
    pooled = jnp.stack(groups, axis=2)
    mixed = jnp.einsum("btgc,gcd->btgd", pooled, pool_w.astype(jnp.float32)).reshape(B, T, W)
    return (mixed * pool_scale.astype(jnp.float32)).astype(u_ext.dtype)


def gla_chunked(q, k, v, log_a, s0):
    B, T, H, Dk = q.shape
    Dv = v.shape[-1]
    C = GLA_CHUNK if T % GLA_CHUNK == 0 else T
    n = T // C

    def chunks(a):
        return a.astype(jnp.float32).reshape(B, n, C, H, a.shape[-1]).transpose(1, 0, 3, 2, 4)

    causal = jnp.tril(jnp.ones((C, C), dtype=bool))[None, None, :, :, None]

    def step(s, inp):
        qi, ki, vi, ai = inp
        b = jnp.cumsum(ai, axis=2)
        o_inter = jnp.einsum("bhtd,bhde->bhte", qi * jnp.exp(b), s)
        decay = jnp.exp(jnp.where(causal, b[:, :, :, None, :] - b[:, :, None, :, :], -jnp.inf))
        scores = jnp.einsum("bhtd,bhsd,bhtsd->bhts", qi, ki, decay)
        o = o_inter + jnp.einsum("bhts,bhse->bhte", scores, vi)
        b_end = b[:, :, -1:, :]
        s_new = jnp.exp(b_end[:, :, 0, :, None]) * s + jnp.einsum("bhsd,bhse->bhde", ki * jnp.exp(b_end - b), vi)
        return s_new, o

    s_fin, o = lax.scan(step, s0.astype(jnp.float32), (chunks(q), chunks(k), chunks(v), chunks(log_a)))
    return o.transpose(1, 0, 3, 2, 4).reshape(B, T, H, Dv), s_fin


def mixer_layer(x, k_past, v_past, pool_prev, gla_prev, norm_g, w_in, sb_qnorm_g, sb_knorm_g, sb_bias,
                pool_w, pool_scale, gla_w2, gla_b2, gla_onorm_g, w_pa, w_pb, w_pc, w_o):
    B, T, _ = x.shape
    start = k_past.shape[1]
    h = rms_norm(x, norm_g)
    (sb_q, sb_k, sb_v, sb_g, pl_u, pl_g, gl_q, gl_k, gl_v, gl_g, gl_r, mg) = jnp.split(h @ w_in, _split_points(), axis=-1)

    q = rms_norm(sb_q.reshape(B, T, SB_HEADS, SB_HEAD_DIM), sb_qnorm_g)
    k = rms_norm(sb_k.reshape(B, T, SB_HEADS, SB_HEAD_DIM), sb_knorm_g)
    v = sb_v.reshape(B, T, SB_HEADS, SB_HEAD_DIM)
    k_all = jnp.concatenate([k_past.astype(k.dtype), k], axis=1)
    v_all = jnp.concatenate([v_past.astype(v.dtype), v], axis=1)
    q_pos = start + jnp.arange(T, dtype=jnp.int32)
    k_pos = jnp.arange(start + T, dtype=jnp.int32)
    o_a = stick_breaking_attention(q, k_all, v_all, q_pos, k_pos, sb_bias).reshape(B, T, SB_WIDTH) * jax.nn.silu(sb_g)

    u_ext = jnp.concatenate([pool_prev.astype(pl_u.dtype), pl_u], axis=1)
    o_b = multiscale_pool(u_ext, start, pool_w, pool_scale) * jax.nn.silu(pl_g)

    log_a = jax.nn.log_sigmoid((gl_r @ gla_w2 + gla_b2).astype(jnp.float32)) / GLA_TAU
    o_c, s_new = gla_chunked(gl_q.reshape(B, T, GLA_HEADS, GLA_DK) * (GLA_DK ** -0.5),
                             gl_k.reshape(B, T, GLA_HEADS, GLA_DK),
                             gl_v.reshape(B, T, GLA_HEADS, GLA_DV),
                             log_a.reshape(B, T, GLA_HEADS, GLA_DK), gla_prev)
    o_c = rms_norm(o_c, gla_onorm_g).reshape(B, T, GLA_VAL_WIDTH).astype(x.dtype) * jax.nn.silu(gl_g)

    gates = jax.nn.sigmoid(mg).reshape(B, T, N_BRANCH, D_MODEL)
    merged = gates[:, :, 0] * (o_a @ w_pa) + gates[:, :, 1] * (o_b @ w_pb) + gates[:, :, 2] * (o_c @ w_pc)
    y = x + merged @ w_o
    return y, k, v, u_ext[:, -POOL_STATE:], s_new.astype(gla_prev.dtype)


def setup_inputs(seed: int = 0) -> dict:
    key = jax.random.key(seed)
    ks = jax.random.split(key, 24)
    n_pages = PAST_LEN // PAGE_SIZE
    n_used = DEC_BATCH * n_pages
    n_pool = n_used + n_used // 4

    def nrm(k, shape, s=1.0):
        return s * jax.random.normal(k, shape, jnp.float32)

    page_table = jax.random.permutation(ks[0], n_pool)[:n_used].reshape(DEC_BATCH, n_pages).astype(jnp.int32)
    sb_bias = -jnp.linspace(SB_BIAS_MIN, SB_BIAS_MAX, SB_HEADS, dtype=jnp.float32)[None, :] + nrm(ks[20], (DEPTH, SB_HEADS), 0.1)
    return {
        "x_prompt": nrm(ks[1], (BATCH, SEQ, D_MODEL)),
        "x_sample": nrm(ks[2], (DEC_BATCH, DEC_SEQ, D_MODEL)),
        "cache_k": nrm(ks[3], (DEPTH, n_pool, PAGE_SIZE, SB_HEADS, SB_HEAD_DIM)),
        "cache_v": nrm(ks[4], (DEPTH, n_pool, PAGE_SIZE, SB_HEADS, SB_HEAD_DIM)),
        "state_pool": nrm(ks[5], (DEPTH, DEC_BATCH, POOL_STATE, POOL_WIDTH)),
        "state_gla": nrm(ks[6], (DEPTH, DEC_BATCH, GLA_HEADS, GLA_DK, GLA_DV), 0.5),
        "page_table": page_table,
        "norm_g": 1.0 + nrm(ks[7], (DEPTH, D_MODEL), 0.1),
        "w_in": nrm(ks[8], (DEPTH, D_MODEL, N_IN), D_MODEL ** -0.5),
        "sb_qnorm_g": 1.0 + nrm(ks[9], (DEPTH, SB_HEAD_DIM), 0.1),
        "sb_knorm_g": 1.0 + nrm(ks[10], (DEPTH, SB_HEAD_DIM), 0.1),
        "sb_bias": sb_bias,
        "pool_w": nrm(ks[11], (DEPTH, POOL_GROUPS, POOL_GROUP_DIM, POOL_GROUP_DIM), POOL_GROUP_DIM ** -0.5),
        "pool_scale": 1.0 + nrm(ks[12], (DEPTH, POOL_WIDTH), 0.1),
        "gla_w2": nrm(ks[13], (DEPTH, GLA_RANK, GLA_KEY_WIDTH), GLA_RANK ** -0.5),
        "gla_b2": nrm(ks[14], (DEPTH, GLA_KEY_WIDTH), 0.1),
        "gla_onorm_g": 1.0 + nrm(ks[15], (DEPTH, GLA_DV), 0.1),
        "w_pa": nrm(ks[16], (DEPTH, SB_WIDTH, D_MODEL), SB_WIDTH ** -0.5),
        "w_pb": nrm(ks[17], (DEPTH, POOL_WIDTH, D_MODEL), POOL_WIDTH ** -0.5),
        "w_pc": nrm(ks[18], (DEPTH, GLA_VAL_WIDTH, D_MODEL), GLA_VAL_WIDTH ** -0.5),
        "w_o": nrm(ks[19], (DEPTH, D_MODEL, D_MODEL), D_MODEL ** -0.5),
    }


def reference(x_prompt, x_sample, cache_k, cache_v, state_pool, state_gla, page_table, norm_g, w_in,
              sb_qnorm_g, sb_knorm_g, sb_bias, pool_w, pool_scale, gla_w2, gla_b2, gla_onorm_g, w_pa, w_pb, w_pc, w_o):
    bp = x_prompt.shape[0]
    bs, n_pages = page_table.shape
    past = n_pages * cache_k.shape[2]
    y_p, y_s = x_prompt, x_sample
    kp_l, vp_l, pp_l, gp_l = [], [], [], []
    ks_l, vs_l, ps_l, gs_l = [], [], [], []
    for l in range(DEPTH):
        weights = (norm_g[l], w_in[l], sb_qnorm_g[l], sb_knorm_g[l], sb_bias[l], pool_w[l], pool_scale[l],
                   gla_w2[l], gla_b2[l], gla_onorm_g[l], w_pa[l], w_pb[l], w_pc[l], w_o[l])
        empty = jnp.zeros((bp, 0, SB_HEADS, SB_HEAD_DIM), x_prompt.dtype)
        y_p, kp, vp, pp, gp = mixer_layer(
            y_p, empty, empty,
            jnp.zeros((bp, POOL_STATE, POOL_WIDTH), x_prompt.dtype),
            jnp.zeros((bp, GLA_HEADS, GLA_DK, GLA_DV), state_gla.dtype), *weights)
        k_past = cache_k[l][page_table].reshape(bs, past, SB_HEADS, SB_HEAD_DIM)
        v_past = cache_v[l][page_table].reshape(bs, past, SB_HEADS, SB_HEAD_DIM)
        y_s, ksm, vsm, psm, gsm = mixer_layer(y_s, k_past, v_past, state_pool[l], state_gla[l], *weights)
        kp_l.append(kp); vp_l.append(vp); pp_l.append(pp); gp_l.append(gp)
        ks_l.append(ksm); vs_l.append(vsm); ps_l.append(psm); gs_l.append(gsm)
    return (y_p, y_s, jnp.stack(kp_l), jnp.stack(vp_l), jnp.stack(pp_l), jnp.stack(gp_l),
            jnp.stack(ks_l), jnp.stack(vs_l), jnp.stack(ps_l), jnp.stack(gs_l))
```

```python
import functools
import math

import jax
import jax.numpy as jnp
from jax import lax
from jax.experimental import pallas as pl
from jax.experimental.pallas import tpu as pltpu

F32 = jnp.float32
BF16 = jnp.bfloat16

D_MODEL = 1024
SB_HEADS = 8
SB_HEAD_DIM = 64
SB_WIDTH = SB_HEADS * SB_HEAD_DIM
POOL_WINDOWS = (2, 4, 8, 16)
POOL_GROUPS = 4
POOL_WIDTH = 512
POOL_GROUP_DIM = POOL_WIDTH // POOL_GROUPS
POOL_STATE = max(POOL_WINDOWS) - 1
POOL_HALO = 16
GLA_HEADS = 4
GLA_DK = 128
GLA_DV = 256
GLA_KEY_WIDTH = GLA_HEADS * GLA_DK
GLA_VAL_WIDTH = GLA_HEADS * GLA_DV
GLA_RANK = 16
GLA_TAU = 16.0
N_BRANCH = 3
EPS = 1e-6
LANES = 128
SUBLANES = 8
N_MAIN = 4 * SB_WIDTH + 2 * POOL_WIDTH + 2 * GLA_KEY_WIDTH + 2 * GLA_VAL_WIDTH
VMEM_LIMIT = 48 * 1024 * 1024

_NT = (((1,), (1,)), ((), ()))
_TN = (((0,), (0,)), ((), ()))


def _dot(a, b):
    return jnp.dot(a, b, preferred_element_type=F32)


def _dot_nt(a, b):
    return lax.dot_general(a, b, _NT, preferred_element_type=F32)


def _silu(g):
    return g / (1.0 + jnp.exp(-g))


def _softplus(z):
    return jnp.maximum(z, 0.0) + jnp.log(1.0 + jnp.exp(-jnp.abs(z)))


def _split_bf16(x):
    hi = x.astype(BF16)
    lo = (x - hi.astype(F32)).astype(BF16)
    return hi, lo


def _params(*sem):
    return pltpu.CompilerParams(dimension_semantics=sem, vmem_limit_bytes=VMEM_LIMIT)


def _proj_kernel(x_ref, ng_ref, w_ref, wr_ref, w2_ref, b2_ref, qg_ref, kg_ref, hm_ref,
                 q_ref, k_ref, v_ref, kb_ref, vb_ref, sbg_ref, plu_ref, plg_ref,
                 glq_ref, glk_ref, glv_ref, glg_ref, la_ref):
    x = x_ref[...]
    ms = jnp.mean(x * x, axis=-1, keepdims=True)
    h = (x * lax.rsqrt(ms + EPS) * ng_ref[...]).astype(BF16)

    def cols(start, width):
        return _dot(h, w_ref[:, start:start + width])

    def head_norm(y, g):
        hi, lo = _split_bf16(y * y)
        m = _dot(hi, hm_ref[...]) + _dot(lo, hm_ref[...])
        return y * lax.rsqrt(m + EPS) * g

    qn = head_norm(cols(0, SB_WIDTH), qg_ref[...])
    q_ref[...] = (qn * (SB_HEAD_DIM ** -0.5)).astype(BF16)
    kn = head_norm(cols(SB_WIDTH, SB_WIDTH), kg_ref[...])
    k_ref[...] = kn
    kb_ref[...] = kn.astype(BF16)
    v = cols(2 * SB_WIDTH, SB_WIDTH)
    v_ref[...] = v
    vb_ref[...] = v.astype(BF16)
    sbg_ref[...] = cols(3 * SB_WIDTH, SB_WIDTH).astype(BF16)
    off = 4 * SB_WIDTH
    plu_ref[...] = cols(off, POOL_WIDTH)
    plg_ref[...] = cols(off + POOL_WIDTH, POOL_WIDTH).astype(BF16)
    off += 2 * POOL_WIDTH
    glq_ref[...] = cols(off, GLA_KEY_WIDTH).astype(BF16)
    glk_ref[...] = cols(off + GLA_KEY_WIDTH, GLA_KEY_WIDTH).astype(BF16)
    off += 2 * GLA_KEY_WIDTH
    glv_ref[...] = cols(off, GLA_VAL_WIDTH).astype(BF16)
    glg_ref[...] = cols(off + GLA_VAL_WIDTH, GLA_VAL_WIDTH).astype(BF16)
    r = _dot(h, wr_ref[...])
    pre = _dot(r.astype(BF16), w2_ref[...]) + b2_ref[...]
    la_ref[...] = (jnp.minimum(pre, 0.0) - jnp.log(1.0 + jnp.exp(-jnp.abs(pre)))) * (1.0 / GLA_TAU)


def _proj(x, lw, tm):
    m = x.shape[0]
    assert m % tm == 0
    row = lambda w: pl.BlockSpec((tm, w), lambda i: (i, 0))
    full = lambda a: pl.BlockSpec(a.shape, lambda i: (0,) * a.ndim)
    consts = (lw["norm_g"], lw["w_main"], lw["w_r"], lw["w2"], lw["b2"], lw["qg"], lw["kg"], lw["head_mean"])
    widths = (SB_WIDTH, SB_WIDTH, SB_WIDTH, SB_WIDTH, SB_WIDTH, SB_WIDTH, POOL_WIDTH, POOL_WIDTH,
              GLA_KEY_WIDTH, GLA_KEY_WIDTH, GLA_VAL_WIDTH, GLA_VAL_WIDTH, GLA_KEY_WIDTH)
    dtypes = (BF16, F32, F32, BF16, BF16, BF16, F32, BF16, BF16, BF16, BF16, BF16, F32)
    outs = pl.pallas_call(
        _proj_kernel,
        grid=(m // tm,),
        in_specs=[row(D_MODEL)] + [full(a) for a in consts],
        out_specs=[row(w) for w in widths],
        out_shape=[jax.ShapeDtypeStruct((m, w), d) for w, d in zip(widths, dtypes)],
        compiler_params=_params("parallel"),
        name="proj",
    )(x, *consts)
    names = ("q", "k", "v", "kb", "vb", "sbg", "plu", "plg", "glq", "glk", "glv", "glg", "la")
    return dict(zip(names, outs))


def _attn_prompt_kernel(bias_ref, q_ref, k_ref, v_ref, o_ref, *, tq, tk):
    hp = pl.program_id(1)
    i = pl.program_id(2)
    q = q_ref[...]
    lane = lax.broadcasted_iota(jnp.int32, (tq, LANES), 1)
    tri = jnp.where(lax.broadcasted_iota(jnp.int32, (tk, tk), 0) > lax.broadcasted_iota(jnp.int32, (tk, tk), 1),
                    1.0, 0.0).astype(BF16)
    q_pos = i * tq + lax.broadcasted_iota(jnp.int32, (tq, tk), 0)
    k_off = lax.broadcasted_iota(jnp.int32, (tq, tk), 1)
    kd = (i * tq) // tk

    def tile(qh, bias, kb, acc, c, masked):
        start = pl.multiple_of(kb * tk, tk)
        ks = k_ref[pl.ds(start, tk), :]
        vs = v_ref[pl.ds(start, tk), :]
        z = _dot_nt(qh, ks) + bias
        sp = _softplus(z)
        if masked:
            valid = (kb * tk + k_off) < q_pos
            sp = jnp.where(valid, sp, 0.0)
        later = _dot(sp.astype(BF16), tri)
        a = jnp.exp((z - sp) - later - c)
        if masked:
            a = jnp.where(valid, a, 0.0)
        acc = acc + _dot(a.astype(BF16), vs)
        c = c + (later[:, 0:1] + sp[:, 0:1])
        return acc, c

    heads = []
    for hh in range(2):
        qh = jnp.where((lane >= SB_HEAD_DIM) == (hh == 1), q, jnp.zeros_like(q))
        bias = bias_ref[2 * hp + hh]
        acc, c = tile(qh, bias, kd, jnp.zeros((tq, LANES), F32), jnp.zeros((tq, 1), F32), True)
        acc, c = lax.fori_loop(
            0, kd, lambda j, carry: tile(qh, bias, kd - 1 - j, carry[0], carry[1], False), (acc, c))
        heads.append(acc)
    o_ref[...] = jnp.where(lane < SB_HEAD_DIM, heads[0], heads[1]).astype(o_ref.dtype)


def _attn_prompt(q, k, v, bias, batch, seq, tq=128, tk=256):
    assert seq % tk == 0 and tk % tq == 0
    nq = seq // tq
    n_pair = SB_WIDTH // LANES
    return pl.pallas_call(
        functools.partial(_attn_prompt_kernel, tq=tq, tk=tk),
        grid=(batch, n_pair, nq),
        in_specs=[
            pl.BlockSpec(memory_space=pltpu.SMEM),
            pl.BlockSpec((tq, LANES), lambda b, p, i: (b * nq + i, p)),
            pl.BlockSpec((seq, LANES), lambda b, p, i: (b, p)),
            pl.BlockSpec((seq, LANES), lambda b, p, i: (b, p)),
        ],
        out_specs=pl.BlockSpec((tq, LANES), lambda b, p, i: (b * nq + i, p)),
        out_shape=jax.ShapeDtypeStruct((batch * seq, SB_WIDTH), BF16),
        compiler_params=_params("parallel", "parallel", "arbitrary"),
        name="attn_prompt",
    )(bias, q, k, v)


def _attn_sample_kernel(pt_ref, bias_ref, q_ref, kn_ref, vn_ref, kc_ref, vc_ref, o_ref, acc_ref, c_ref,
                        *, page, t_pad):
    del pt_ref
    j = pl.program_id(1)
    rows = SB_HEADS * t_pad
    row = lax.broadcasted_iota(jnp.int32, (rows, SB_WIDTH), 0)
    lane = lax.broadcasted_iota(jnp.int32, (rows, SB_WIDTH), 1)
    own = (lane // SB_HEAD_DIM) == (row // t_pad)
    q8 = q_ref[...]
    qbd = jnp.where(own, jnp.concatenate([q8] * SB_HEADS, axis=0), jnp.zeros((rows, SB_WIDTH), q8.dtype))
    tri = jnp.where(lax.broadcasted_iota(jnp.int32, (page, page), 0) > lax.broadcasted_iota(jnp.int32, (page, page), 1),
                    1.0, 0.0).astype(BF16)

    def tile(ks, vs, valid):
        z = _dot_nt(qbd, ks) + bias_ref[...]
        sp = _softplus(z)
        if valid is not None:
            sp = jnp.where(valid, sp, 0.0)
        later = _dot(sp.astype(BF16), tri)
        a = jnp.exp((z - sp) - later - c_ref[...])
        if valid is not None:
            a = jnp.where(valid, a, 0.0)
        acc_ref[...] += _dot(a.astype(BF16), vs)
        c_ref[...] += later[:, 0:1] + sp[:, 0:1]

    @pl.when(j == 0)
    def _():
        acc_ref[...] = jnp.zeros_like(acc_ref)
        c_ref[...] = jnp.zeros_like(c_ref)
        pad = jnp.zeros((page - t_pad, SB_WIDTH), BF16)
        key_idx = lax.broadcasted_iota(jnp.int32, (rows, page), 1)
        qry_idx = lax.broadcasted_iota(jnp.int32, (rows, page), 0) % t_pad
        tile(jnp.concatenate([kn_ref[...], pad], axis=0), jnp.concatenate([vn_ref[...], pad], axis=0),
             key_idx < qry_idx)

    @pl.when(j > 0)
    def _():
        tile(kc_ref[...].astype(BF16), vc_ref[...].astype(BF16), None)

    @pl.when(j == pl.num_programs(1) - 1)
    def _():
        picked = jnp.where(own, acc_ref[...], 0.0)
        o_ref[...] = picked.reshape(SB_HEADS, t_pad, SB_WIDTH).sum(axis=0).astype(o_ref.dtype)


def _attn_sample(q, kn, vn, cache_k, cache_v, page_table, bias_rows, base, t_pad):
    batch = q.shape[0] // t_pad
    n_pages = page_table.shape[0] // batch
    page = cache_k.shape[1]

    def page_map(b, j, pt):
        return (base + pt[b * n_pages + n_pages - jnp.maximum(j, 1)], 0, 0)

    new_spec = pl.BlockSpec((t_pad, SB_WIDTH), lambda b, j, pt: (b, 0))
    grid_spec = pltpu.PrefetchScalarGridSpec(
        num_scalar_prefetch=1,
        grid=(batch, n_pages + 1),
        in_specs=[
            pl.BlockSpec(bias_rows.shape, lambda b, j, pt: (0, 0)),
            new_spec, new_spec, new_spec,
            pl.BlockSpec((None, page, SB_WIDTH), page_map),
            pl.BlockSpec((None, page, SB_WIDTH), page_map),
        ],
        out_specs=new_spec,
        scratch_shapes=[pltpu.VMEM((SB_HEADS * t_pad, SB_WIDTH), F32), pltpu.VMEM((SB_HEADS * t_pad, page), F32)],
    )
    return pl.pallas_call(
        functools.partial(_attn_sample_kernel, page=page, t_pad=t_pad),
        grid_spec=grid_spec,
        out_shape=jax.ShapeDtypeStruct(q.shape, BF16),
        compiler_params=_params("parallel", "arbitrary"),
        name="attn_sample",
    )(page_table, bias_rows, q, kn, vn, cache_k, cache_v)


def _pool_mix(pooled_groups, pw_ref, scale_ref, gate):
    mixed = jnp.concatenate([_dot(p.astype(BF16), pw_ref[g]) for g, p in enumerate(pooled_groups)], axis=1)
    return mixed * scale_ref[...] * _silu(gate)


def _pool_prompt_kernel(u_ref, halo_ref, g_ref, pw_ref, scale_ref, o_ref, *, tm, seq):
    i = pl.program_id(0)
    start = (i * tm) % seq
    u = u_ref[...]
    halo = jnp.where(start == 0, 0.0, halo_ref[...])
    ext = jnp.concatenate([halo, u], axis=0)
    pos = start + lax.broadcasted_iota(jnp.int32, (tm, 1), 0)
    pooled = []
    for g, w in enumerate(POOL_WINDOWS):
        s = ext[:, g * POOL_GROUP_DIM:(g + 1) * POOL_GROUP_DIM]
        k = 1
        while k < w:
            s = s + pltpu.roll(s, k, 0)
            k *= 2
        count = jnp.minimum(w, pos + 1).astype(F32)
        pooled.append(s[POOL_HALO:] / count - u[:, g * POOL_GROUP_DIM:(g + 1) * POOL_GROUP_DIM])
    o_ref[...] = _pool_mix(pooled, pw_ref, scale_ref, g_ref[...].astype(F32)).astype(o_ref.dtype)


def _pool_prompt(u, gate, pool_w, pool_scale, seq, tm):
    m = u.shape[0]
    assert seq % tm == 0 and tm % POOL_HALO == 0
    ratio = tm // POOL_HALO
    return pl.pallas_call(
        functools.partial(_pool_prompt_kernel, tm=tm, seq=seq),
        grid=(m // tm,),
        in_specs=[
            pl.BlockSpec((tm, POOL_WIDTH), lambda i: (i, 0)),
            pl.BlockSpec((POOL_HALO, POOL_WIDTH), lambda i: (jnp.maximum(i * ratio - 1, 0), 0)),
            pl.BlockSpec((tm, POOL_WIDTH), lambda i: (i, 0)),
            pl.BlockSpec(pool_w.shape, lambda i: (0, 0, 0)),
            pl.BlockSpec(pool_scale.shape, lambda i: (0, 0)),
        ],
        out_specs=pl.BlockSpec((tm, POOL_WIDTH), lambda i: (i, 0)),
        out_shape=jax.ShapeDtypeStruct((m, POOL_WIDTH), BF16),
        compiler_params=_params("parallel"),
        name="pool_prompt",
    )(u, u, gate, pool_w, pool_scale)


def _pool_sample_kernel(ue_ref, g_ref, pw_ref, scale_ref, o_ref, *, t_new):
    batch = ue_ref.shape[1]
    pooled = []
    for g, w in enumerate(POOL_WINDOWS):
        lanes = slice(g * POOL_GROUP_DIM, (g + 1) * POOL_GROUP_DIM)
        per_t = []
        for t in range(t_new):
            cur = POOL_STATE + t
            s = ue_ref[cur - w + 1, :, lanes]
            for r in range(cur - w + 2, cur + 1):
                s = s + ue_ref[r, :, lanes]
            per_t.append(s / float(w) - ue_ref[cur, :, lanes])
        pooled.append(jnp.concatenate(per_t, axis=0))
    gate = g_ref[...].reshape(t_new * batch, POOL_WIDTH).astype(F32)
    o_ref[...] = _pool_mix(pooled, pw_ref, scale_ref, gate).reshape(t_new, batch, POOL_WIDTH).astype(o_ref.dtype)


def _pool_sample(u_ext_t, gate_t, pool_w, pool_scale):
    t_new = gate_t.shape[0]
    return pl.pallas_call(
        functools.partial(_pool_sample_kernel, t_new=t_new),
        out_shape=jax.ShapeDtypeStruct(gate_t.shape, BF16),
        compiler_params=pltpu.CompilerParams(vmem_limit_bytes=VMEM_LIMIT),
        name="pool_sample",
    )(u_ext_t, gate_t, pool_w, pool_scale)


def _block_reference_rows(b, bs_ref, m, t_idx):
    c = b.shape[0]
    if 2 * m >= SUBLANES:
        parts = [jnp.broadcast_to(bs_ref[p * 2 * m + m - 1:p * 2 * m + m, :], (2 * m, b.shape[1]))
                 for p in range(c // (2 * m))]
        return parts[0] if len(parts) == 1 else jnp.concatenate(parts, axis=0)
    if m == 2:
        t4 = t_idx & 3
        return jnp.where(t4 == 0, pltpu.roll(b, c - 1, 0),
                         jnp.where(t4 == 1, b, jnp.where(t4 == 2, pltpu.roll(b, 1, 0), pltpu.roll(b, 2, 0))))
    assert m == 1
    return jnp.where((t_idx & 1) == 0, b, pltpu.roll(b, 1, 0))


def _gla_kernel(q_ref, k_ref, v_ref, g_ref, la_ref, og_ref, s0_ref, o_ref, sout_ref, s_ref, bs_ref, *, chunk):
    ci = pl.program_id(1)

    @pl.when(ci == 0)
    def _():
        s_ref[...] = s0_ref[...]

    t_row = lax.broadcasted_iota(jnp.int32, (chunk, chunk), 0)
    t_col = lax.broadcasted_iota(jnp.int32, (chunk, chunk), 1)
    t_idx = lax.broadcasted_iota(jnp.int32, (chunk, GLA_DK), 0)
    tril = jnp.where(t_row >= t_col, 1.0, 0.0).astype(BF16)
    pair_diff = t_row ^ t_col

    for h in range(GLA_HEADS):
        kl = slice(h * GLA_DK, (h + 1) * GLA_DK)
        vl = slice(h * GLA_DV, (h + 1) * GLA_DV)
        hi, lo = _split_bf16(la_ref[:, kl])
        b = _dot(tril, hi) + _dot(tril, lo)
        bs_ref[...] = b
        q = q_ref[:, kl].astype(F32) * (GLA_DK ** -0.5)
        k = k_ref[:, kl].astype(F32)
        v = v_ref[:, vl]
        s = s_ref[h]

        o = _dot((q * jnp.exp(b)).astype(BF16), s.astype(BF16))
        scores = jnp.where(pair_diff == 0, _dot_nt(q.astype(BF16), k.astype(BF16)), 0.0)
        m = chunk // 2
        while m >= 1:
            e = jnp.exp(-jnp.abs(b - _block_reference_rows(b, bs_ref, m, t_idx)))
            second = (t_idx & m) != 0
            qf = jnp.where(second, q * e, 0.0).astype(BF16)
            kf = jnp.where(second, 0.0, k * e).astype(BF16)
            scores = scores + jnp.where(pair_diff < 2 * m, _dot_nt(qf, kf), 0.0)
            m //= 2
        o = o + _dot(scores.astype(BF16), v)

        b_end = b[chunk - 1:chunk, :]
        k_end = (k * jnp.exp(b_end - b)).astype(BF16)
        upd = lax.dot_general(k_end, v, _TN, preferred_element_type=F32)
        decay_col = jnp.broadcast_to(jnp.exp(b_end), (GLA_DK, GLA_DK)).T
        s_ref[h] = s * jnp.concatenate([decay_col] * (GLA_DV // GLA_DK), axis=1) + upd

        on = o * lax.rsqrt(jnp.mean(o * o, axis=-1, keepdims=True) + EPS) * og_ref[...]
        o_ref[:, vl] = (on * _silu(g_ref[:, vl].astype(F32))).astype(o_ref.dtype)

    @pl.when(ci == pl.num_programs(1) - 1)
    def _():
        sout_ref[...] = s_ref[...]


def _gla(q, k, v, g, la, onorm_g, s0, chunk):
    batch = s0.shape[0]
    nc = q.shape[0] // (batch * chunk)
    tok = lambda w: pl.BlockSpec((chunk, w), lambda b, c: (b * nc + c, 0))
    state = pl.BlockSpec((None, GLA_HEADS, GLA_DK, GLA_DV), lambda b, c: (b, 0, 0, 0))
    return pl.pallas_call(
        functools.partial(_gla_kernel, chunk=chunk),
        grid=(batch, nc),
        in_specs=[tok(GLA_KEY_WIDTH), tok(GLA_KEY_WIDTH), tok(GLA_VAL_WIDTH), tok(GLA_VAL_WIDTH), tok(GLA_KEY_WIDTH),
                  pl.BlockSpec(onorm_g.shape, lambda b, c: (0, 0)), state],
        out_specs=[tok(GLA_VAL_WIDTH), state],
        out_shape=[jax.ShapeDtypeStruct(v.shape, BF16), jax.ShapeDtypeStruct(s0.shape, F32)],
        scratch_shapes=[pltpu.VMEM((GLA_HEADS, GLA_DK, GLA_DV), F32), pltpu.VMEM((chunk, GLA_DK), F32)],
        compiler_params=_params("parallel", "arbitrary"),
        name="gla",
    )(q, k, v, g, la, onorm_g, s0)


def _merge_kernel(x_ref, ng_ref, wg_ref, oa_ref, sbg_ref, ob_ref, oc_ref, wa_ref, wb_ref, wc_ref, wo_ref, y_ref):
    x = x_ref[...]
    ms = jnp.mean(x * x, axis=-1, keepdims=True)
    h = (x * lax.rsqrt(ms + EPS) * ng_ref[...]).astype(BF16)
    a = (oa_ref[...].astype(F32) * _silu(sbg_ref[...].astype(F32))).astype(BF16)
    branches = (_dot(a, wa_ref[...]), _dot(ob_ref[...], wb_ref[...]), _dot(oc_ref[...], wc_ref[...]))
    merged = None
    for n, p in enumerate(branches):
        gate = 1.0 / (1.0 + jnp.exp(-_dot(h, wg_ref[:, n * D_MODEL:(n + 1) * D_MODEL])))
        merged = gate * p if merged is None else merged + gate * p
    y_ref[...] = x + _dot(merged.astype(BF16), wo_ref[...])


def _merge(x, oa, sbg, ob, oc, lw, tm):
    m = x.shape[0]
    assert m % tm == 0
    row = lambda w: pl.BlockSpec((tm, w), lambda i: (i, 0))
    full = lambda a: pl.BlockSpec(a.shape, lambda i: (0,) * a.ndim)
    return pl.pallas_call(
        _merge_kernel,
        grid=(m // tm,),
        in_specs=[row(D_MODEL), full(lw["norm_g"]), full(lw["w_mg"]), row(SB_WIDTH), row(SB_WIDTH), row(POOL_WIDTH),
                  row(GLA_VAL_WIDTH), full(lw["w_pa"]), full(lw["w_pb"]), full(lw["w_pc"]), full(lw["w_o"])],
        out_specs=row(D_MODEL),
        out_shape=jax.ShapeDtypeStruct((m, D_MODEL), F32),
        compiler_params=_params("parallel"),
        name="merge",
    )(x, lw["norm_g"], lw["w_mg"], oa, sbg, ob, oc, lw["w_pa"], lw["w_pb"], lw["w_pc"], lw["w_o"])


def _layer_weights(l, norm_g, w_in, sb_qnorm_g, sb_knorm_g, pool_w, pool_scale, gla_w2, gla_b2, gla_onorm_g,
                   w_pa, w_pb, w_pc, w_o):
    w = w_in[l]
    head = jnp.arange(SB_WIDTH, dtype=jnp.int32) // SB_HEAD_DIM
    return {
        "norm_g": norm_g[l][None, :],
        "w_main": w[:, :N_MAIN].astype(BF16),
        "w_r": jnp.pad(w[:, N_MAIN:N_MAIN + GLA_RANK], ((0, 0), (0, LANES - GLA_RANK))).astype(BF16),
        "w_mg": w[:, N_MAIN + GLA_RANK:].astype(BF16),
        "w2": jnp.pad(gla_w2[l], ((0, LANES - GLA_RANK), (0, 0))).astype(BF16),
        "b2": gla_b2[l][None, :],
        "qg": jnp.tile(sb_qnorm_g[l], SB_HEADS)[None, :],
        "kg": jnp.tile(sb_knorm_g[l], SB_HEADS)[None, :],
        "head_mean": jnp.where(head[:, None] == head[None, :], 1.0 / SB_HEAD_DIM, 0.0).astype(BF16),
        "pool_w": pool_w[l].astype(BF16),
        "pool_scale": pool_scale[l][None, :],
        "onorm_g": gla_onorm_g[l][None, :],
        "w_pa": w_pa[l].astype(BF16),
        "w_pb": w_pb[l].astype(BF16),
        "w_pc": w_pc[l].astype(BF16),
        "w_o": w_o[l].astype(BF16),
    }


def _layer_prompt(x, lw, sb_bias_l, batch, seq, tm=256, gla_chunk=128):
    p = _proj(x, lw, tm)
    oa = _attn_prompt(p["q"], p["kb"], p["vb"], sb_bias_l, batch, seq)
    ob = _pool_prompt(p["plu"], p["plg"], lw["pool_w"], lw["pool_scale"], seq, tm)
    s0 = jnp.zeros((batch, GLA_HEADS, GLA_DK, GLA_DV), F32)
    oc, s_new = _gla(p["glq"], p["glk"], p["glv"], p["glg"], p["la"], lw["onorm_g"], s0, gla_chunk)
    y = _merge(x, oa, p["sbg"], ob, oc, lw, tm)
    k = p["k"].reshape(batch, seq, SB_HEADS, SB_HEAD_DIM)
    v = p["v"].reshape(batch, seq, SB_HEADS, SB_HEAD_DIM)
    pool_state = p["plu"].reshape(batch, seq, POOL_WIDTH)[:, seq - POOL_STATE:]
    return y, k, v, pool_state, s_new


def _layer_sample(x, lw, sb_bias_l, cache_k, cache_v, page_table, pool_prev, gla_prev, base, batch, t_new, tm):
    t_pad = SUBLANES
    assert t_new <= t_pad and page_table.shape[0] // batch * cache_k.shape[1] >= POOL_STATE

    def pad_t(a):
        a = a.reshape(batch, t_new, a.shape[-1])
        return jnp.pad(a, ((0, 0), (0, t_pad - t_new), (0, 0))).reshape(batch * t_pad, a.shape[-1])

    def unpad_t(a):
        return a.reshape(batch, t_pad, a.shape[-1])[:, :t_new].reshape(batch * t_new, a.shape[-1])

    p = _proj(x, lw, tm)
    bias_rows = jnp.broadcast_to(jnp.repeat(sb_bias_l, t_pad)[:, None], (SB_HEADS * t_pad, cache_k.shape[1]))
    oa = unpad_t(_attn_sample(pad_t(p["q"]), pad_t(p["kb"]), pad_t(p["vb"]), cache_k, cache_v, page_table,
                              bias_rows, base, t_pad))

    u_new = p["plu"].reshape(batch, t_new, POOL_WIDTH)
    u_ext = jnp.concatenate([pool_prev, u_new], axis=1)
    gate_t = p["plg"].reshape(batch, t_new, POOL_WIDTH).transpose(1, 0, 2)
    ob = _pool_sample(u_ext.transpose(1, 0, 2), gate_t, lw["pool_w"], lw["pool_scale"])
    ob = ob.transpose(1, 0, 2).reshape(batch * t_new, POOL_WIDTH)

    oc, s_new = _gla(pad_t(p["glq"]), pad_t(p["glk"]), pad_t(p["glv"]), pad_t(p["glg"]), pad_t(p["la"]),
                     lw["onorm_g"], gla_prev, t_pad)
    y = _merge(x, oa, p["sbg"], ob, unpad_t(oc), lw, tm)
    k = p["k"].reshape(batch, t_new, SB_HEADS, SB_HEAD_DIM)
    v = p["v"].reshape(batch, t_new, SB_HEADS, SB_HEAD_DIM)
    return y, k, v, u_ext[:, t_new:], s_new


def kernel(x_prompt, x_sample, cache_k, cache_v, state_pool, state_gla, page_table, norm_g, w_in, sb_qnorm_g,
           sb_knorm_g, sb_bias, pool_w, pool_scale, gla_w2, gla_b2, gla_onorm_g, w_pa, w_pb, w_pc, w_o):
    bp, seq, _ = x_prompt.shape
    bs, t_new, _ = x_sample.shape
    depth, n_pool, page = cache_k.shape[:3]
    ck = cache_k.reshape(depth * n_pool, page, SB_WIDTH)
    cv = cache_v.reshape(depth * n_pool, page, SB_WIDTH)
    pt = page_table.reshape(-1).astype(jnp.int32)
    y_p = x_prompt.reshape(bp * seq, D_MODEL)
    y_s = x_sample.reshape(bs * t_new, D_MODEL)
    outs_p, outs_s = [], []
    for l in range(depth):
        lw = _layer_weights(l, norm_g, w_in, sb_qnorm_g, sb_knorm_g, pool_w, pool_scale, gla_w2, gla_b2,
                            gla_onorm_g, w_pa, w_pb, w_pc, w_o)
        y_p, *rest_p = _layer_prompt(y_p, lw, sb_bias[l], bp, seq)
        y_s, *rest_s = _layer_sample(y_s, lw, sb_bias[l], ck, cv, pt, state_pool[l], state_gla[l], l * n_pool, bs, t_new,
                                     tm=min(256, bs * t_new))
        outs_p.append(rest_p)
        outs_s.append(rest_s)
    stack = lambda outs, n: jnp.stack([o[n] for o in outs])
    return (y_p.reshape(bp, seq, D_MODEL), y_s.reshape(bs, t_new, D_MODEL),
            stack(outs_p, 0), stack(outs_p, 1), stack(outs_p, 2), stack(outs_p, 3),
            stack(outs_s, 0), stack(outs_s, 1), stack(outs_s, 2), stack(outs_s, 3))
```

```python
import functools
import math

import jax
import jax.numpy as jnp
from jax import lax
from jax.experimental import pallas as pl
from jax.experimental.pallas import tpu as pltpu

F32 = jnp.float32
BF16 = jnp.bfloat16

D_MODEL = 1024
SB_HEADS = 8
SB_HEAD_DIM = 64
SB_WIDTH = SB_HEADS * SB_HEAD_DIM
POOL_WINDOWS = (2, 4, 8, 16)
POOL_GROUPS = 4
POOL_WIDTH = 512
POOL_GROUP_DIM = POOL_WIDTH // POOL_GROUPS
POOL_STATE = max(POOL_WINDOWS) - 1
POOL_HALO = 16
GLA_HEADS = 4
GLA_DK = 128
GLA_DV = 256
GLA_KEY_WIDTH = GLA_HEADS * GLA_DK
GLA_VAL_WIDTH = GLA_HEADS * GLA_DV
GLA_RANK = 16
GLA_TAU = 16.0
N_BRANCH = 3
EPS = 1e-6
LOG2E = math.log2(math.e)
LANES = 128
SUBLANES = 8
N_MAIN = 4 * SB_WIDTH + 2 * POOL_WIDTH + 2 * GLA_KEY_WIDTH + 2 * GLA_VAL_WIDTH
VMEM_LIMIT = 48 * 1024 * 1024

_NT = (((1,), (1,)), ((), ()))
_TN = (((0,), (0,)), ((), ()))


def _dot(a, b):
    return jnp.dot(a, b, preferred_element_type=F32)


def _dot_nt(a, b):
    return lax.dot_general(a, b, _NT, preferred_element_type=F32)


def _silu(g):
    return g / (1.0 + jnp.exp(-g))


def _softplus2(z):
    neg_abs = lax.bitcast_convert_type(lax.bitcast_convert_type(z, jnp.uint32) | jnp.uint32(0x80000000), F32)
    return jnp.maximum(z, 0.0) + jnp.log(1.0 + jnp.exp2(neg_abs)) * LOG2E


def _split_bf16(x):
    hi = x.astype(BF16)
    lo = (x - hi.astype(F32)).astype(BF16)
    return hi, lo


def _params(*sem):
    return pltpu.CompilerParams(dimension_semantics=sem, vmem_limit_bytes=VMEM_LIMIT)


def _proj_kernel(x_ref, ng_ref, w_ref, wr_ref, w2_ref, b2_ref, qg_ref, kg_ref, hm_ref,
                 q_ref, k_ref, v_ref, kb_ref, vb_ref, sbg_ref, plu_ref, plg_ref,
                 glq_ref, glk_ref, glv_ref, glg_ref, la_ref):
    x = x_ref[...]
    ms = jnp.mean(x * x, axis=-1, keepdims=True)
    h = (x * lax.rsqrt(ms + EPS) * ng_ref[...]).astype(BF16)

    def cols(start, width):
        return _dot(h, w_ref[:, start:start + width])

    def head_norm(y, g):
        hi, lo = _split_bf16(y * y)
        m = _dot(hi, hm_ref[...]) + _dot(lo, hm_ref[...])
        return y * lax.rsqrt(m + EPS) * g

    qn = head_norm(cols(0, SB_WIDTH), qg_ref[...])
    q_ref[...] = (qn * (LOG2E * SB_HEAD_DIM ** -0.5)).astype(BF16)
    kn = head_norm(cols(SB_WIDTH, SB_WIDTH), kg_ref[...])
    k_ref[...] = kn
    kb_ref[...] = kn.astype(BF16)
    v = cols(2 * SB_WIDTH, SB_WIDTH)
    v_ref[...] = v
    vb_ref[...] = v.astype(BF16)
    sbg_ref[...] = cols(3 * SB_WIDTH, SB_WIDTH).astype(BF16)
    off = 4 * SB_WIDTH
    plu_ref[...] = cols(off, POOL_WIDTH)
    plg_ref[...] = cols(off + POOL_WIDTH, POOL_WIDTH).astype(BF16)
    off += 2 * POOL_WIDTH
    glq_ref[...] = cols(off, GLA_KEY_WIDTH).astype(BF16)
    glk_ref[...] = cols(off + GLA_KEY_WIDTH, GLA_KEY_WIDTH).astype(BF16)
    off += 2 * GLA_KEY_WIDTH
    glv_ref[...] = cols(off, GLA_VAL_WIDTH).astype(BF16)
    glg_ref[...] = cols(off + GLA_VAL_WIDTH, GLA_VAL_WIDTH).astype(BF16)
    r = _dot(h, wr_ref[...])
    pre = _dot(r.astype(BF16), w2_ref[...]) + b2_ref[...]
    la_ref[...] = (jnp.minimum(pre, 0.0) - jnp.log(1.0 + jnp.exp(-jnp.abs(pre)))) * (1.0 / GLA_TAU)


def _proj(x, lw, tm):
    m = x.shape[0]
    assert m % tm == 0
    row = lambda w: pl.BlockSpec((tm, w), lambda i: (i, 0))
    full = lambda a: pl.BlockSpec(a.shape, lambda i: (0,) * a.ndim)
    consts = (lw["norm_g"], lw["w_main"], lw["w_r"], lw["w2"], lw["b2"], lw["qg"], lw["kg"], lw["head_mean"])
    widths = (SB_WIDTH, SB_WIDTH, SB_WIDTH, SB_WIDTH, SB_WIDTH, SB_WIDTH, POOL_WIDTH, POOL_WIDTH,
              GLA_KEY_WIDTH, GLA_KEY_WIDTH, GLA_VAL_WIDTH, GLA_VAL_WIDTH, GLA_KEY_WIDTH)
    dtypes = (BF16, F32, F32, BF16, BF16, BF16, F32, BF16, BF16, BF16, BF16, BF16, F32)
    outs = pl.pallas_call(
        _proj_kernel,
        grid=(m // tm,),
        in_specs=[row(D_MODEL)] + [full(a) for a in consts],
        out_specs=[row(w) for w in widths],
        out_shape=[jax.ShapeDtypeStruct((m, w), d) for w, d in zip(widths, dtypes)],
        compiler_params=_params("parallel"),
        name="proj",
    )(x, *consts)
    names = ("q", "k", "v", "kb", "vb", "sbg", "plu", "plg", "glq", "glk", "glv", "glg", "la")
    return dict(zip(names, outs))


def _attn_prompt_kernel(bias_ref, q_ref, k_ref, v_ref, o_ref, *, tq, tk, unroll):
    hp = pl.program_id(1)
    i = pl.program_id(2)
    q = q_ref[...]
    lane = lax.broadcasted_iota(jnp.int32, (tq, LANES), 1)
    tri = jnp.where(lax.broadcasted_iota(jnp.int32, (tk, tk), 0) > lax.broadcasted_iota(jnp.int32, (tk, tk), 1),
                    1.0, 0.0).astype(BF16)
    kd = (i * tq) // tk
    qhs = [jnp.where((lane >= SB_HEAD_DIM) == (hh == 1), q, jnp.zeros_like(q)) for hh in range(2)]
    biases = [bias_ref[2 * hp + hh] for hh in range(2)]

    def front(hh, kb, valid):
        ks = k_ref[pl.ds(pl.multiple_of(kb * tk, tk), tk), :]
        z = _dot_nt(qhs[hh], ks) + biases[hh]
        sp = _softplus2(z)
        if valid is not None:
            sp = jnp.where(valid, sp, 0.0)
        later = _dot(sp.astype(BF16), tri)
        return (z - sp) - later, later[:, 0:1] + sp[:, 0:1]

    def back(pre, tot, kb, valid, acc, c):
        vs = v_ref[pl.ds(pl.multiple_of(kb * tk, tk), tk), :]
        a = jnp.exp2(pre - c)
        if valid is not None:
            a = jnp.where(valid, a, 0.0)
        return acc + _dot(a.astype(BF16), vs), c + tot

    def blocks(kbs, valid, carry):
        fronts = [[front(hh, kb, valid) for hh in range(2)] for kb in kbs]
        carry = list(carry)
        for kb, fr in zip(kbs, fronts):
            for hh in range(2):
                carry[hh] = back(*fr[hh], kb, valid, *carry[hh])
        return tuple(carry)

    q_pos = i * tq + lax.broadcasted_iota(jnp.int32, (tq, tk), 0)
    k_pos = kd * tk + lax.broadcasted_iota(jnp.int32, (tq, tk), 1)
    zero = (jnp.zeros((tq, LANES), F32), jnp.zeros((tq, 1), F32))
    carry = blocks([kd], k_pos < q_pos, (zero, zero))
    carry = lax.fori_loop(
        0, kd // unroll,
        lambda j, cr: blocks([kd - 1 - j * unroll - u for u in range(unroll)], None, cr), carry)
    rest = kd % unroll
    carry = lax.fori_loop(0, rest, lambda j, cr: blocks([rest - 1 - j], None, cr), carry)
    o_ref[...] = jnp.where(lane < SB_HEAD_DIM, carry[0][0], carry[1][0]).astype(o_ref.dtype)


def _attn_prompt(q, k, v, bias, batch, seq, tq=256, tk=256, unroll=4):
    assert seq % tk == 0 and tk % tq == 0
    nq = seq // tq
    n_pair = SB_WIDTH // LANES
    return pl.pallas_call(
        functools.partial(_attn_prompt_kernel, tq=tq, tk=tk, unroll=unroll),
        grid=(batch, n_pair, nq),
        in_specs=[
            pl.BlockSpec(memory_space=pltpu.SMEM),
            pl.BlockSpec((tq, LANES), lambda b, p, i: (b * nq + i, p)),
            pl.BlockSpec((seq, LANES), lambda b, p, i: (b, p)),
            pl.BlockSpec((seq, LANES), lambda b, p, i: (b, p)),
        ],
        out_specs=pl.BlockSpec((tq, LANES), lambda b, p, i: (b * nq + i, p)),
        out_shape=jax.ShapeDtypeStruct((batch * seq, SB_WIDTH), BF16),
        compiler_params=_params("parallel", "parallel", "arbitrary"),
        name="attn_prompt",
    )(bias, q, k, v)


def _attn_sample_kernel(pt_ref, bias_ref, q_ref, kn_ref, vn_ref, kc_hbm, vc_hbm, o_ref, kbuf, vbuf, sems,
                        *, page, t_pad, n_pages, base, kblk):
    b = pl.program_id(0)
    nb = pl.num_programs(0)
    slot = b % 2

    def page_copies(seq, slot_):
        copies = []
        for p in range(n_pages):
            idx = base + pt_ref[seq * n_pages + p]
            dst = pl.ds(p * page, page)
            copies.append(pltpu.make_async_copy(kc_hbm.at[idx], kbuf.at[slot_, :, dst], sems.at[slot_, 0, p]))
            copies.append(pltpu.make_async_copy(vc_hbm.at[idx], vbuf.at[slot_, :, dst], sems.at[slot_, 1, p]))
        return copies

    @pl.when(b == 0)
    def _():
        for cp in page_copies(0, 0):
            cp.start()

    @pl.when(b + 1 < nb)
    def _():
        for cp in page_copies(b + 1, 1 - slot):
            cp.start()

    rows = SB_HEADS * t_pad
    row = lax.broadcasted_iota(jnp.int32, (rows, SB_WIDTH), 0)
    lane = lax.broadcasted_iota(jnp.int32, (rows, SB_WIDTH), 1)
    own = (lane // SB_HEAD_DIM) == (row // t_pad)
    q8 = q_ref[...]
    qbd = jnp.where(own, jnp.concatenate([q8] * SB_HEADS, axis=0), jnp.zeros((rows, SB_WIDTH), q8.dtype))

    def strict_tri(n):
        return jnp.where(lax.broadcasted_iota(jnp.int32, (n, n), 0) > lax.broadcasted_iota(jnp.int32, (n, n), 1),
                         1.0, 0.0).astype(BF16)

    pad = jnp.zeros((page - t_pad, SB_WIDTH), BF16)
    key_idx = lax.broadcasted_iota(jnp.int32, (rows, page), 1)
    qry_idx = lax.broadcasted_iota(jnp.int32, (rows, page), 0) % t_pad
    valid = key_idx < qry_idx
    z = _dot_nt(qbd, jnp.concatenate([kn_ref[...], pad], axis=0)) + bias_ref[...]
    sp = jnp.where(valid, _softplus2(z), 0.0)
    later = _dot(sp.astype(BF16), strict_tri(page))
    a = jnp.where(valid, jnp.exp2((z - sp) - later), 0.0)
    acc = _dot(a.astype(BF16), jnp.concatenate([vn_ref[...], pad], axis=0))
    c = later[:, 0:1] + sp[:, 0:1]

    for cp in page_copies(b, slot):
        cp.wait()

    past = n_pages * page
    z = _dot(qbd, kbuf[slot].astype(BF16)) + bias_ref[:, 0:1]
    sp = _softplus2(z)
    tri = strict_tri(kblk)
    a_blocks = [None] * (past // kblk)
    for j in reversed(range(past // kblk)):
        cols = slice(j * kblk, (j + 1) * kblk)
        later = _dot(sp[:, cols].astype(BF16), tri)
        a_blocks[j] = jnp.exp2((z[:, cols] - sp[:, cols]) - later - c).astype(BF16)
        c = c + (later[:, 0:1] + sp[:, j * kblk:j * kblk + 1])
    acc = acc + _dot_nt(jnp.concatenate(a_blocks, axis=1), vbuf[slot].astype(BF16))

    picked = jnp.where(own, acc, 0.0)
    o_ref[...] = picked.reshape(SB_HEADS, t_pad, SB_WIDTH).sum(axis=0).astype(o_ref.dtype)


def _attn_sample(q, kn, vn, cache_kt, cache_vt, page_table, bias_rows, base, t_pad, kblk=256):
    batch = q.shape[0] // t_pad
    n_pages = page_table.shape[0] // batch
    page = cache_kt.shape[2]
    kblk = min(kblk, n_pages * page)
    assert (n_pages * page) % kblk == 0
    new_spec = pl.BlockSpec((t_pad, SB_WIDTH), lambda b, pt: (b, 0))
    hbm = pl.BlockSpec(memory_space=pl.ANY)
    grid_spec = pltpu.PrefetchScalarGridSpec(
        num_scalar_prefetch=1,
        grid=(batch,),
        in_specs=[pl.BlockSpec(bias_rows.shape, lambda b, pt: (0, 0)), new_spec, new_spec, new_spec, hbm, hbm],
        out_specs=new_spec,
        scratch_shapes=[pltpu.VMEM((2, SB_WIDTH, n_pages * page), F32), pltpu.VMEM((2, SB_WIDTH, n_pages * page), F32),
                        pltpu.SemaphoreType.DMA((2, 2, n_pages))],
    )
    return pl.pallas_call(
        functools.partial(_attn_sample_kernel, page=page, t_pad=t_pad, n_pages=n_pages, base=base, kblk=kblk),
        grid_spec=grid_spec,
        out_shape=jax.ShapeDtypeStruct(q.shape, BF16),
        compiler_params=_params("arbitrary"),
        name="attn_sample",
    )(page_table, bias_rows, q, kn, vn, cache_kt, cache_vt)


def _pool_mix(pooled_groups, pw_ref, scale_ref, gate):
    mixed = jnp.concatenate([_dot(p.astype(BF16), pw_ref[g]) for g, p in enumerate(pooled_groups)], axis=1)
    return mixed * scale_ref[...] * _silu(gate)


def _pool_prompt_kernel(u_ref, halo_ref, g_ref, pw_ref, scale_ref, o_ref, *, tm, seq):
    i = pl.program_id(0)
    start = (i * tm) % seq
    u = u_ref[...]
    halo = jnp.where(start == 0, 0.0, halo_ref[...])
    ext = jnp.concatenate([halo, u], axis=0)
    pos = start + lax.broadcasted_iota(jnp.int32, (tm, 1), 0)
    pooled = []
    for g, w in enumerate(POOL_WINDOWS):
        s = ext[:, g * POOL_GROUP_DIM:(g + 1) * POOL_GROUP_DIM]
        k = 1
        while k < w:
            s = s + pltpu.roll(s, k, 0)
            k *= 2
        count = jnp.minimum(w, pos + 1).astype(F32)
        pooled.append(s[POOL_HALO:] / count - u[:, g * POOL_GROUP_DIM:(g + 1) * POOL_GROUP_DIM])
    o_ref[...] = _pool_mix(pooled, pw_ref, scale_ref, g_ref[...].astype(F32)).astype(o_ref.dtype)


def _pool_prompt(u, gate, pool_w, pool_scale, seq, tm):
    m = u.shape[0]
    assert seq % tm == 0 and tm % POOL_HALO == 0
    ratio = tm // POOL_HALO
    return pl.pallas_call(
        functools.partial(_pool_prompt_kernel, tm=tm, seq=seq),
        grid=(m // tm,),
        in_specs=[
            pl.BlockSpec((tm, POOL_WIDTH), lambda i: (i, 0)),
            pl.BlockSpec((POOL_HALO, POOL_WIDTH), lambda i: (jnp.maximum(i * ratio - 1, 0), 0)),
            pl.BlockSpec((tm, POOL_WIDTH), lambda i: (i, 0)),
            pl.BlockSpec(pool_w.shape, lambda i: (0, 0, 0)),
            pl.BlockSpec(pool_scale.shape, lambda i: (0, 0)),
        ],
        out_specs=pl.BlockSpec((tm, POOL_WIDTH), lambda i: (i, 0)),
        out_shape=jax.ShapeDtypeStruct((m, POOL_WIDTH), BF16),
        compiler_params=_params("parallel"),
        name="pool_prompt",
    )(u, u, gate, pool_w, pool_scale)


def _pool_sample_kernel(ue_ref, g_ref, pw_ref, scale_ref, o_ref, *, t_new):
    batch = ue_ref.shape[1]
    pooled = []
    for g, w in enumerate(POOL_WINDOWS):
        lanes = slice(g * POOL_GROUP_DIM, (g + 1) * POOL_GROUP_DIM)
        per_t = []
        for t in range(t_new):
            cur = POOL_STATE + t
            s = ue_ref[cur - w + 1, :, lanes]
            for r in range(cur - w + 2, cur + 1):
                s = s + ue_ref[r, :, lanes]
            per_t.append(s / float(w) - ue_ref[cur, :, lanes])
        pooled.append(jnp.concatenate(per_t, axis=0))
    gate = g_ref[...].reshape(t_new * batch, POOL_WIDTH).astype(F32)
    o_ref[...] = _pool_mix(pooled, pw_ref, scale_ref, gate).reshape(t_new, batch, POOL_WIDTH).astype(o_ref.dtype)


def _pool_sample(u_ext_t, gate_t, pool_w, pool_scale):
    t_new = gate_t.shape[0]
    return pl.pallas_call(
        functools.partial(_pool_sample_kernel, t_new=t_new),
        out_shape=jax.ShapeDtypeStruct(gate_t.shape, BF16),
        compiler_params=pltpu.CompilerParams(vmem_limit_bytes=VMEM_LIMIT),
        name="pool_sample",
    )(u_ext_t, gate_t, pool_w, pool_scale)


def _block_reference_rows(b, bs_ref, m, t_idx):
    c = b.shape[0]
    if 2 * m >= SUBLANES:
        parts = [jnp.broadcast_to(bs_ref[p * 2 * m + m - 1:p * 2 * m + m, :], (2 * m, b.shape[1]))
                 for p in range(c // (2 * m))]
        return parts[0] if len(parts) == 1 else jnp.concatenate(parts, axis=0)
    if m == 2:
        t4 = t_idx & 3
        return jnp.where(t4 == 0, pltpu.roll(b, c - 1, 0),
                         jnp.where(t4 == 1, b, jnp.where(t4 == 2, pltpu.roll(b, 1, 0), pltpu.roll(b, 2, 0))))
    assert m == 1
    return jnp.where((t_idx & 1) == 0, b, pltpu.roll(b, 1, 0))


def _gla_kernel(q_ref, k_ref, v_ref, g_ref, la_ref, og_ref, s0_ref, o_ref, sout_ref, s_ref, bs_ref, *, chunk):
    ci = pl.program_id(1)

    @pl.when(ci == 0)
    def _():
        s_ref[...] = s0_ref[...]

    t_row = lax.broadcasted_iota(jnp.int32, (chunk, chunk), 0)
    t_col = lax.broadcasted_iota(jnp.int32, (chunk, chunk), 1)
    t_idx = lax.broadcasted_iota(jnp.int32, (chunk, GLA_DK), 0)
    tril = jnp.where(t_row >= t_col, 1.0, 0.0).astype(BF16)
    pair_diff = t_row ^ t_col

    for h in range(GLA_HEADS):
        kl = slice(h * GLA_DK, (h + 1) * GLA_DK)
        vl = slice(h * GLA_DV, (h + 1) * GLA_DV)
        hi, lo = _split_bf16(la_ref[:, kl])
        b = _dot(tril, hi) + _dot(tril, lo)
        bs_ref[...] = b
        q = q_ref[:, kl].astype(F32) * (GLA_DK ** -0.5)
        k = k_ref[:, kl].astype(F32)
        v = v_ref[:, vl]
        s = s_ref[h]

        o = _dot((q * jnp.exp(b)).astype(BF16), s.astype(BF16))
        scores = jnp.where(pair_diff == 0, _dot_nt(q.astype(BF16), k.astype(BF16)), 0.0)
        m = chunk // 2
        while m >= 1:
            e = jnp.exp(-jnp.abs(b - _block_reference_rows(b, bs_ref, m, t_idx)))
            second = (t_idx & m) != 0
            qf = jnp.where(second, q * e, 0.0).astype(BF16)
            kf = jnp.where(second, 0.0, k * e).astype(BF16)
            scores = scores + jnp.where(pair_diff < 2 * m, _dot_nt(qf, kf), 0.0)
            m //= 2
        o = o + _dot(scores.astype(BF16), v)

        b_end = b[chunk - 1:chunk, :]
        k_end = (k * jnp.exp(b_end - b)).astype(BF16)
        upd = lax.dot_general(k_end, v, _TN, preferred_element_type=F32)
        decay_col = jnp.broadcast_to(jnp.exp(b_end), (GLA_DK, GLA_DK)).T
        s_ref[h] = s * jnp.concatenate([decay_col] * (GLA_DV // GLA_DK), axis=1) + upd

        on = o * lax.rsqrt(jnp.mean(o * o, axis=-1, keepdims=True) + EPS) * og_ref[...]
        o_ref[:, vl] = (on * _silu(g_ref[:, vl].astype(F32))).astype(o_ref.dtype)

    @pl.when(ci == pl.num_programs(1) - 1)
    def _():
        sout_ref[...] = s_ref[...]


def _gla(q, k, v, g, la, onorm_g, s0, chunk):
    batch = s0.shape[0]
    nc = q.shape[0] // (batch * chunk)
    tok = lambda w: pl.BlockSpec((chunk, w), lambda b, c: (b * nc + c, 0))
    state = pl.BlockSpec((None, GLA_HEADS, GLA_DK, GLA_DV), lambda b, c: (b, 0, 0, 0))
    return pl.pallas_call(
        functools.partial(_gla_kernel, chunk=chunk),
        grid=(batch, nc),
        in_specs=[tok(GLA_KEY_WIDTH), tok(GLA_KEY_WIDTH), tok(GLA_VAL_WIDTH), tok(GLA_VAL_WIDTH), tok(GLA_KEY_WIDTH),
                  pl.BlockSpec(onorm_g.shape, lambda b, c: (0, 0)), state],
        out_specs=[tok(GLA_VAL_WIDTH), state],
        out_shape=[jax.ShapeDtypeStruct(v.shape, BF16), jax.ShapeDtypeStruct(s0.shape, F32)],
        scratch_shapes=[pltpu.VMEM((GLA_HEADS, GLA_DK, GLA_DV), F32), pltpu.VMEM((chunk, GLA_DK), F32)],
        compiler_params=_params("parallel", "arbitrary"),
        name="gla",
    )(q, k, v, g, la, onorm_g, s0)


def _merge_kernel(x_ref, ng_ref, wg_ref, oa_ref, sbg_ref, ob_ref, oc_ref, wa_ref, wb_ref, wc_ref, wo_ref, y_ref):
    x = x_ref[...]
    ms = jnp.mean(x * x, axis=-1, keepdims=True)
    h = (x * lax.rsqrt(ms + EPS) * ng_ref[...]).astype(BF16)
    a = (oa_ref[...].astype(F32) * _silu(sbg_ref[...].astype(F32))).astype(BF16)
    branches = (_dot(a, wa_ref[...]), _dot(ob_ref[...], wb_ref[...]), _dot(oc_ref[...], wc_ref[...]))
    merged = None
    for n, p in enumerate(branches):
        gate = 1.0 / (1.0 + jnp.exp(-_dot(h, wg_ref[:, n * D_MODEL:(n + 1) * D_MODEL])))
        merged = gate * p if merged is None else merged + gate * p
    y_ref[...] = x + _dot(merged.astype(BF16), wo_ref[...])


def _merge(x, oa, sbg, ob, oc, lw, tm):
    m = x.shape[0]
    assert m % tm == 0
    row = lambda w: pl.BlockSpec((tm, w), lambda i: (i, 0))
    full = lambda a: pl.BlockSpec(a.shape, lambda i: (0,) * a.ndim)
    return pl.pallas_call(
        _merge_kernel,
        grid=(m // tm,),
        in_specs=[row(D_MODEL), full(lw["norm_g"]), full(lw["w_mg"]), row(SB_WIDTH), row(SB_WIDTH), row(POOL_WIDTH),
                  row(GLA_VAL_WIDTH), full(lw["w_pa"]), full(lw["w_pb"]), full(lw["w_pc"]), full(lw["w_o"])],
        out_specs=row(D_MODEL),
        out_shape=jax.ShapeDtypeStruct((m, D_MODEL), F32),
        compiler_params=_params("parallel"),
        name="merge",
    )(x, lw["norm_g"], lw["w_mg"], oa, sbg, ob, oc, lw["w_pa"], lw["w_pb"], lw["w_pc"], lw["w_o"])


def _layer_weights(l, norm_g, w_in, sb_qnorm_g, sb_knorm_g, pool_w, pool_scale, gla_w2, gla_b2, gla_onorm_g,
                   w_pa, w_pb, w_pc, w_o):
    w = w_in[l]
    head = jnp.arange(SB_WIDTH, dtype=jnp.int32) // SB_HEAD_DIM
    return {
        "norm_g": norm_g[l][None, :],
        "w_main": w[:, :N_MAIN].astype(BF16),
        "w_r": jnp.pad(w[:, N_MAIN:N_MAIN + GLA_RANK], ((0, 0), (0, LANES - GLA_RANK))).astype(BF16),
        "w_mg": w[:, N_MAIN + GLA_RANK:].astype(BF16),
        "w2": jnp.pad(gla_w2[l], ((0, LANES - GLA_RANK), (0, 0))).astype(BF16),
        "b2": gla_b2[l][None, :],
        "qg": jnp.tile(sb_qnorm_g[l], SB_HEADS)[None, :],
        "kg": jnp.tile(sb_knorm_g[l], SB_HEADS)[None, :],
        "head_mean": jnp.where(head[:, None] == head[None, :], 1.0 / SB_HEAD_DIM, 0.0).astype(BF16),
        "pool_w": pool_w[l].astype(BF16),
        "pool_scale": pool_scale[l][None, :],
        "onorm_g": gla_onorm_g[l][None, :],
        "w_pa": w_pa[l].astype(BF16),
        "w_pb": w_pb[l].astype(BF16),
        "w_pc": w_pc[l].astype(BF16),
        "w_o": w_o[l].astype(BF16),
    }


def _layer_prompt(x, lw, sb_bias_l, batch, seq, tm=256, gla_chunk=128):
    p = _proj(x, lw, tm)
    oa = _attn_prompt(p["q"], p["kb"], p["vb"], sb_bias_l * LOG2E, batch, seq)
    ob = _pool_prompt(p["plu"], p["plg"], lw["pool_w"], lw["pool_scale"], seq, tm)
    s0 = jnp.zeros((batch, GLA_HEADS, GLA_DK, GLA_DV), F32)
    oc, s_new = _gla(p["glq"], p["glk"], p["glv"], p["glg"], p["la"], lw["onorm_g"], s0, gla_chunk)
    y = _merge(x, oa, p["sbg"], ob, oc, lw, tm)
    k = p["k"].reshape(batch, seq, SB_HEADS, SB_HEAD_DIM)
    v = p["v"].reshape(batch, seq, SB_HEADS, SB_HEAD_DIM)
    pool_state = p["plu"].reshape(batch, seq, POOL_WIDTH)[:, seq - POOL_STATE:]
    return y, k, v, pool_state, s_new


def _layer_sample(x, lw, sb_bias_l, cache_k, cache_v, page_table, pool_prev, gla_prev, base, batch, t_new, tm):
    t_pad = SUBLANES
    page = cache_k.shape[2]
    assert t_new <= t_pad and page_table.shape[0] // batch * page >= POOL_STATE

    def pad_t(a):
        a = a.reshape(batch, t_new, a.shape[-1])
        return jnp.pad(a, ((0, 0), (0, t_pad - t_new), (0, 0))).reshape(batch * t_pad, a.shape[-1])

    def unpad_t(a):
        return a.reshape(batch, t_pad, a.shape[-1])[:, :t_new].reshape(batch * t_new, a.shape[-1])

    p = _proj(x, lw, tm)
    bias_rows = jnp.broadcast_to(jnp.repeat(sb_bias_l * LOG2E, t_pad)[:, None], (SB_HEADS * t_pad, page))
    oa = unpad_t(_attn_sample(pad_t(p["q"]), pad_t(p["kb"]), pad_t(p["vb"]), cache_k, cache_v, page_table,
                              bias_rows, base, t_pad))

    u_new = p["plu"].reshape(batch, t_new, POOL_WIDTH)
    u_ext = jnp.concatenate([pool_prev, u_new], axis=1)
    gate_t = p["plg"].reshape(batch, t_new, POOL_WIDTH).transpose(1, 0, 2)
    ob = _pool_sample(u_ext.transpose(1, 0, 2), gate_t, lw["pool_w"], lw["pool_scale"])
    ob = ob.transpose(1, 0, 2).reshape(batch * t_new, POOL_WIDTH)

    oc, s_new = _gla(pad_t(p["glq"]), pad_t(p["glk"]), pad_t(p["glv"]), pad_t(p["glg"]), pad_t(p["la"]),
                     lw["onorm_g"], gla_prev, t_pad)
    y = _merge(x, oa, p["sbg"], ob, unpad_t(oc), lw, tm)
    k = p["k"].reshape(batch, t_new, SB_HEADS, SB_HEAD_DIM)
    v = p["v"].reshape(batch, t_new, SB_HEADS, SB_HEAD_DIM)
    return y, k, v, u_ext[:, t_new:], s_new


def kernel(x_prompt, x_sample, cache_k, cache_v, state_pool, state_gla, page_table, norm_g, w_in, sb_qnorm_g,
           sb_knorm_g, sb_bias, pool_w, pool_scale, gla_w2, gla_b2, gla_onorm_g, w_pa, w_pb, w_pc, w_o):
    bp, seq, _ = x_prompt.shape
    bs, t_new, _ = x_sample.shape
    depth, n_pool, page = cache_k.shape[:3]
    ck = cache_k.transpose(0, 1, 3, 4, 2).reshape(depth * n_pool, SB_WIDTH, page)
    cv = cache_v.transpose(0, 1, 3, 4, 2).reshape(depth * n_pool, SB_WIDTH, page)
    pt = page_table.reshape(-1).astype(jnp.int32)
    y_p = x_prompt.reshape(bp * seq, D_MODEL)
    y_s = x_sample.reshape(bs * t_new, D_MODEL)
    outs_p, outs_s = [], []
    for l in range(depth):
        lw = _layer_weights(l, norm_g, w_in, sb_qnorm_g, sb_knorm_g, pool_w, pool_scale, gla_w2, gla_b2,
                            gla_onorm_g, w_pa, w_pb, w_pc, w_o)
        y_p, *rest_p = _layer_prompt(y_p, lw, sb_bias[l], bp, seq)
        y_s, *rest_s = _layer_sample(y_s, lw, sb_bias[l], ck, cv, pt, state_pool[l], state_gla[l], l * n_pool, bs, t_new,
                                     tm=min(256, bs * t_new))
        outs_p.append(rest_p)
        outs_s.append(rest_s)
    stack = lambda outs, n: jnp.stack([o[n] for o in outs])
    return (y_p.reshape(bp, seq, D_MODEL), y_s.reshape(bs, t_new, D_MODEL),
            stack(outs_p, 0), stack(outs_p, 1), stack(outs_p, 2), stack(outs_p, 3),
            stack(outs_s, 0), stack(outs_s, 1), stack(outs_s, 2), stack(outs_s, 3))
```

```python
import functools
import math

import jax
import jax.numpy as jnp
from jax import lax
from jax.experimental import pallas as pl
from jax.experimental.pallas import tpu as pltpu

F32 = jnp.float32
BF16 = jnp.bfloat16

D_MODEL = 1024
SB_HEADS = 8
SB_HEAD_DIM = 64
SB_WIDTH = SB_HEADS * SB_HEAD_DIM
POOL_WINDOWS = (2, 4, 8, 16)
POOL_GROUPS = 4
POOL_WIDTH = 512
POOL_GROUP_DIM = POOL_WIDTH // POOL_GROUPS
POOL_STATE = max(POOL_WINDOWS) - 1
POOL_HALO = 16
GLA_HEADS = 4
GLA_DK = 128
GLA_DV = 256
GLA_KEY_WIDTH = GLA_HEADS * GLA_DK
GLA_VAL_WIDTH = GLA_HEADS * GLA_DV
GLA_RANK = 16
GLA_TAU = 16.0
N_BRANCH = 3
EPS = 1e-6
LOG2E = math.log2(math.e)
LANES = 128
SUBLANES = 8
N_MAIN = 4 * SB_WIDTH + 2 * POOL_WIDTH + 2 * GLA_KEY_WIDTH + 2 * GLA_VAL_WIDTH
VMEM_LIMIT = 48 * 1024 * 1024

_NT = (((1,), (1,)), ((), ()))
_TN = (((0,), (0,)), ((), ()))


def _dot(a, b):
    return jnp.dot(a, b, preferred_element_type=F32)


def _dot_nt(a, b):
    return lax.dot_general(a, b, _NT, preferred_element_type=F32)


def _silu(g):
    return g / (1.0 + jnp.exp(-g))


def _softplus2(z):
    one = jnp.ones((), z.dtype)
    return jnp.maximum(z, jnp.zeros((), z.dtype)) + jnp.log2(one + jnp.exp2(-jnp.abs(z)))


def _split_bf16(x):
    hi = x.astype(BF16)
    lo = (x - hi.astype(F32)).astype(BF16)
    return hi, lo


def _params(*sem):
    return pltpu.CompilerParams(dimension_semantics=sem, vmem_limit_bytes=VMEM_LIMIT)


def _proj_kernel(x_ref, ng_ref, w_ref, wr_ref, w2_ref, b2_ref, qg_ref, kg_ref, hm_ref, kt_all_ref, vt_all_ref,
                 q_ref, kt_ref, vt_ref, kb_ref, vb_ref, sbg_ref, plu_ref, plg_ref,
                 glq_ref, glk_ref, glv_ref, glg_ref, la_ref):
    del kt_all_ref, vt_all_ref
    x = x_ref[...]
    ms = jnp.mean(x * x, axis=-1, keepdims=True)
    h = (x * lax.rsqrt(ms + EPS) * ng_ref[...]).astype(BF16)

    def cols(start, width):
        return _dot(h, w_ref[:, start:start + width])

    def head_norm(y, g):
        hi, lo = _split_bf16(y * y)
        m = _dot(hi, hm_ref[...]) + _dot(lo, hm_ref[...])
        return y * lax.rsqrt(m + EPS) * g

    qn = head_norm(cols(0, SB_WIDTH), qg_ref[...])
    q_ref[...] = (qn * (LOG2E * SB_HEAD_DIM ** -0.5)).astype(BF16)
    kn = head_norm(cols(SB_WIDTH, SB_WIDTH), kg_ref[...])
    kt_ref[...] = kn.T
    kb_ref[...] = kn.astype(BF16)
    v = cols(2 * SB_WIDTH, SB_WIDTH)
    vt_ref[...] = v.T
    vb_ref[...] = v.astype(BF16)
    sbg_ref[...] = cols(3 * SB_WIDTH, SB_WIDTH).astype(BF16)
    off = 4 * SB_WIDTH
    plu_ref[...] = cols(off, POOL_WIDTH)
    plg_ref[...] = cols(off + POOL_WIDTH, POOL_WIDTH).astype(BF16)
    off += 2 * POOL_WIDTH
    glq_ref[...] = cols(off, GLA_KEY_WIDTH).astype(BF16)
    glk_ref[...] = cols(off + GLA_KEY_WIDTH, GLA_KEY_WIDTH).astype(BF16)
    off += 2 * GLA_KEY_WIDTH
    glv_ref[...] = cols(off, GLA_VAL_WIDTH).astype(BF16)
    glg_ref[...] = cols(off + GLA_VAL_WIDTH, GLA_VAL_WIDTH).astype(BF16)
    r = _dot(h, wr_ref[...])
    pre = _dot(r.astype(BF16), w2_ref[...]) + b2_ref[...]
    la_ref[...] = (jnp.minimum(pre, 0.0) - jnp.log(1.0 + jnp.exp(-jnp.abs(pre)))) * (1.0 / GLA_TAU)


def _proj(x, lw, tm, seq, layer, kv_all):
    m = x.shape[0]
    assert m % seq == 0 and seq % tm == 0
    per_seq = seq // tm
    row = lambda w: pl.BlockSpec((tm, w), lambda i: (i, 0))
    full = lambda a: pl.BlockSpec(a.shape, lambda i: (0,) * a.ndim)
    feat = pl.BlockSpec((None, None, SB_WIDTH, tm), lambda i: (layer, i // per_seq, 0, i % per_seq))
    consts = (lw["norm_g"], lw["w_main"], lw["w_r"], lw["w2"], lw["b2"], lw["qg"], lw["kg"], lw["head_mean"])
    widths = (SB_WIDTH, None, None, SB_WIDTH, SB_WIDTH, SB_WIDTH, POOL_WIDTH, POOL_WIDTH,
              GLA_KEY_WIDTH, GLA_KEY_WIDTH, GLA_VAL_WIDTH, GLA_VAL_WIDTH, GLA_KEY_WIDTH)
    dtypes = (BF16, F32, F32, BF16, BF16, BF16, F32, BF16, BF16, BF16, BF16, BF16, F32)
    kv_shape = jax.ShapeDtypeStruct(kv_all[0].shape, F32)
    assert kv_all[0].shape[1:] == (m // seq, SB_WIDTH, seq)
    n_in = 1 + len(consts)
    outs = pl.pallas_call(
        _proj_kernel,
        grid=(m // tm,),
        in_specs=[row(D_MODEL)] + [full(a) for a in consts] + [pl.BlockSpec(memory_space=pl.ANY)] * 2,
        out_specs=[feat if w is None else row(w) for w in widths],
        out_shape=[kv_shape if w is None else jax.ShapeDtypeStruct((m, w), d) for w, d in zip(widths, dtypes)],
        input_output_aliases={n_in: 1, n_in + 1: 2},
        compiler_params=_params("parallel"),
        name="proj",
    )(x, *consts, *kv_all)
    names = ("q", "kt", "vt", "kb", "vb", "sbg", "plu", "plg", "glq", "glk", "glv", "glg", "la")
    return dict(zip(names, outs))


def _attn_prompt_kernel(bias_ref, q_ref, k_ref, v_ref, o_ref, *, tq, tk, unrolls, heads):
    g = pl.program_id(1)
    i = pl.program_id(2)
    lane = lax.broadcasted_iota(jnp.int32, (tq, LANES), 1)
    tri = jnp.where(lax.broadcasted_iota(jnp.int32, (tk, tk), 0) > lax.broadcasted_iota(jnp.int32, (tk, tk), 1),
                    1.0, 0.0).astype(BF16)
    kd = (i * tq) // tk
    qhs, biases = [], []
    for hh in range(heads):
        q = q_ref[:, (hh // 2) * LANES:(hh // 2 + 1) * LANES]
        qhs.append(jnp.where((lane >= SB_HEAD_DIM) == (hh % 2 == 1), q, jnp.zeros_like(q)))
        biases.append(bias_ref[heads * g + hh])

    def front(hh, kb, valid):
        ks = k_ref[pl.ds(pl.multiple_of(kb * tk, tk), tk), (hh // 2) * LANES:(hh // 2 + 1) * LANES]
        z = _dot_nt(qhs[hh], ks) + biases[hh]
        sp = _softplus2(z)
        if valid is not None:
            sp = jnp.where(valid, sp, 0.0)
        later = _dot(sp.astype(BF16), tri)
        return (z - sp) - later, later[:, 0:1] + sp[:, 0:1]

    def back(hh, pre, tot, kb, valid, acc, c):
        vs = v_ref[pl.ds(pl.multiple_of(kb * tk, tk), tk), (hh // 2) * LANES:(hh // 2 + 1) * LANES]
        a = jnp.exp2(pre - c)
        if valid is not None:
            a = jnp.where(valid, a, 0.0)
        return acc + _dot(a.astype(BF16), vs), c + tot

    def blocks(kbs, valid, carry):
        fronts = [[front(hh, kb, valid) for hh in range(heads)] for kb in kbs]
        carry = list(carry)
        for kb, fr in zip(kbs, fronts):
            for hh in range(heads):
                carry[hh] = back(hh, *fr[hh], kb, valid, *carry[hh])
        return tuple(carry)

    q_pos = i * tq + lax.broadcasted_iota(jnp.int32, (tq, tk), 0)
    k_pos = kd * tk + lax.broadcasted_iota(jnp.int32, (tq, tk), 1)
    zero = (jnp.zeros((tq, LANES), F32), jnp.zeros((tq, 1), F32))
    carry = blocks([kd], k_pos < q_pos, (zero,) * heads)
    left = kd
    for u in unrolls:
        top = left
        carry = lax.fori_loop(
            0, left // u, lambda j, cr: blocks([top - 1 - j * u - r for r in range(u)], None, cr), carry)
        left = left % u
    for t in range(heads // 2):
        o_ref[:, t * LANES:(t + 1) * LANES] = jnp.where(
            lane < SB_HEAD_DIM, carry[2 * t][0], carry[2 * t + 1][0]).astype(o_ref.dtype)


def _attn_prompt(q, k, v, bias, batch, seq, tq=256, tk=256, unrolls=(4, 1), heads=4):
    assert seq % tk == 0 and tk % tq == 0 and unrolls[-1] == 1 and heads % 2 == 0 and SB_HEADS % heads == 0
    nq = seq // tq
    width = heads // 2 * LANES
    return pl.pallas_call(
        functools.partial(_attn_prompt_kernel, tq=tq, tk=tk, unrolls=unrolls, heads=heads),
        grid=(batch, SB_HEADS // heads, nq),
        in_specs=[
            pl.BlockSpec(memory_space=pltpu.SMEM),
            pl.BlockSpec((tq, width), lambda b, g, i: (b * nq + i, g)),
            pl.BlockSpec((seq, width), lambda b, g, i: (b, g)),
            pl.BlockSpec((seq, width), lambda b, g, i: (b, g)),
        ],
        out_specs=pl.BlockSpec((tq, width), lambda b, g, i: (b * nq + i, g)),
        out_shape=jax.ShapeDtypeStruct((batch * seq, SB_WIDTH), BF16),
        compiler_params=_params("parallel", "parallel", "arbitrary"),
        name="attn_prompt",
    )(bias, q, k, v)


def _attn_sample_kernel(pt_ref, bias_ref, q_ref, kn_ref, vn_ref, kc_hbm, vc_hbm, o_ref, kbuf, vbuf, sems,
                        *, page, t_pad, n_pages, base, kblk):
    b = pl.program_id(0)
    nb = pl.num_programs(0)
    slot = b % 2

    def page_copies(seq, slot_):
        copies = []
        for p in range(n_pages):
            idx = base + pt_ref[seq * n_pages + p]
            dst = pl.ds(p * page, page)
            copies.append(pltpu.make_async_copy(kc_hbm.at[idx], kbuf.at[slot_, :, dst], sems.at[slot_, 0, p]))
            copies.append(pltpu.make_async_copy(vc_hbm.at[idx], vbuf.at[slot_, :, dst], sems.at[slot_, 1, p]))
        return copies

    @pl.when(b == 0)
    def _():
        for cp in page_copies(0, 0):
            cp.start()

    @pl.when(b + 1 < nb)
    def _():
        for cp in page_copies(b + 1, 1 - slot):
            cp.start()

    rows = SB_HEADS * t_pad
    row = lax.broadcasted_iota(jnp.int32, (rows, SB_WIDTH), 0)
    lane = lax.broadcasted_iota(jnp.int32, (rows, SB_WIDTH), 1)
    own = (lane // SB_HEAD_DIM) == (row // t_pad)
    q8 = q_ref[...]
    qbd = jnp.where(own, jnp.concatenate([q8] * SB_HEADS, axis=0), jnp.zeros((rows, SB_WIDTH), q8.dtype))

    def strict_tri(n):
        return jnp.where(lax.broadcasted_iota(jnp.int32, (n, n), 0) > lax.broadcasted_iota(jnp.int32, (n, n), 1),
                         1.0, 0.0).astype(BF16)

    pad = jnp.zeros((page - t_pad, SB_WIDTH), BF16)
    key_idx = lax.broadcasted_iota(jnp.int32, (rows, page), 1)
    qry_idx = lax.broadcasted_iota(jnp.int32, (rows, page), 0) % t_pad
    valid = key_idx < qry_idx
    z = _dot_nt(qbd, jnp.concatenate([kn_ref[...], pad], axis=0)) + bias_ref[...]
    sp = jnp.where(valid, _softplus2(z), 0.0)
    later = _dot(sp.astype(BF16), strict_tri(page))
    a = jnp.where(valid, jnp.exp2((z - sp) - later), 0.0)
    acc = _dot(a.astype(BF16), jnp.concatenate([vn_ref[...], pad], axis=0))
    c = later[:, 0:1] + sp[:, 0:1]

    for cp in page_copies(b, slot):
        cp.wait()

    past = n_pages * page
    z = _dot(qbd, kbuf[slot].astype(BF16)) + bias_ref[:, 0:1]
    sp = _softplus2(z)
    tri = strict_tri(kblk)
    a_blocks = [None] * (past // kblk)
    for j in reversed(range(past // kblk)):
        cols = slice(j * kblk, (j + 1) * kblk)
        later = _dot(sp[:, cols].astype(BF16), tri)
        a_blocks[j] = jnp.exp2((z[:, cols] - sp[:, cols]) - later - c).astype(BF16)
        c = c + (later[:, 0:1] + sp[:, j * kblk:j * kblk + 1])
    acc = acc + _dot_nt(jnp.concatenate(a_blocks, axis=1), vbuf[slot].astype(BF16))

    picked = jnp.where(own, acc, 0.0)
    o_ref[...] = picked.reshape(SB_HEADS, t_pad, SB_WIDTH).sum(axis=0).astype(o_ref.dtype)


def _attn_sample(q, kn, vn, cache_kt, cache_vt, page_table, bias_rows, base, t_pad, kblk=256):
    batch = q.shape[0] // t_pad
    n_pages = page_table.shape[0] // batch
    page = cache_kt.shape[2]
    kblk = min(kblk, n_pages * page)
    assert (n_pages * page) % kblk == 0
    new_spec = pl.BlockSpec((t_pad, SB_WIDTH), lambda b, pt: (b, 0))
    hbm = pl.BlockSpec(memory_space=pl.ANY)
    grid_spec = pltpu.PrefetchScalarGridSpec(
        num_scalar_prefetch=1,
        grid=(batch,),
        in_specs=[pl.BlockSpec(bias_rows.shape, lambda b, pt: (0, 0)), new_spec, new_spec, new_spec, hbm, hbm],
        out_specs=new_spec,
        scratch_shapes=[pltpu.VMEM((2, SB_WIDTH, n_pages * page), F32), pltpu.VMEM((2, SB_WIDTH, n_pages * page), F32),
                        pltpu.SemaphoreType.DMA((2, 2, n_pages))],
    )
    return pl.pallas_call(
        functools.partial(_attn_sample_kernel, page=page, t_pad=t_pad, n_pages=n_pages, base=base, kblk=kblk),
        grid_spec=grid_spec,
        out_shape=jax.ShapeDtypeStruct(q.shape, BF16),
        compiler_params=_params("arbitrary"),
        name="attn_sample",
    )(page_table, bias_rows, q, kn, vn, cache_kt, cache_vt)


def _pool_mix(pooled_groups, pw_ref, scale_ref, gate):
    mixed = jnp.concatenate([_dot(p.astype(BF16), pw_ref[g]) for g, p in enumerate(pooled_groups)], axis=1)
    return mixed * scale_ref[...] * _silu(gate)


def _pool_prompt_kernel(u_ref, halo_ref, g_ref, pw_ref, scale_ref, o_ref, *, tm, seq):
    i = pl.program_id(0)
    start = (i * tm) % seq
    u = u_ref[...]
    halo = jnp.where(start == 0, 0.0, halo_ref[...])
    ext = jnp.concatenate([halo, u], axis=0)
    pos = start + lax.broadcasted_iota(jnp.int32, (tm, 1), 0)
    pooled = []
    for g, w in enumerate(POOL_WINDOWS):
        s = ext[:, g * POOL_GROUP_DIM:(g + 1) * POOL_GROUP_DIM]
        k = 1
        while k < w:
            s = s + pltpu.roll(s, k, 0)
            k *= 2
        count = jnp.minimum(w, pos + 1).astype(F32)
        pooled.append(s[POOL_HALO:] / count - u[:, g * POOL_GROUP_DIM:(g + 1) * POOL_GROUP_DIM])
    o_ref[...] = _pool_mix(pooled, pw_ref, scale_ref, g_ref[...].astype(F32)).astype(o_ref.dtype)


def _pool_prompt(u, gate, pool_w, pool_scale, seq, tm):
    m = u.shape[0]
    assert seq % tm == 0 and tm % POOL_HALO == 0
    ratio = tm // POOL_HALO
    return pl.pallas_call(
        functools.partial(_pool_prompt_kernel, tm=tm, seq=seq),
        grid=(m // tm,),
        in_specs=[
            pl.BlockSpec((tm, POOL_WIDTH), lambda i: (i, 0)),
            pl.BlockSpec((POOL_HALO, POOL_WIDTH), lambda i: (jnp.maximum(i * ratio - 1, 0), 0)),
            pl.BlockSpec((tm, POOL_WIDTH), lambda i: (i, 0)),
            pl.BlockSpec(pool_w.shape, lambda i: (0, 0, 0)),
            pl.BlockSpec(pool_scale.shape, lambda i: (0, 0)),
        ],
        out_specs=pl.BlockSpec((tm, POOL_WIDTH), lambda i: (i, 0)),
        out_shape=jax.ShapeDtypeStruct((m, POOL_WIDTH), BF16),
        compiler_params=_params("parallel"),
        name="pool_prompt",
    )(u, u, gate, pool_w, pool_scale)


def _pool_sample_kernel(ue_ref, g_ref, pw_ref, scale_ref, o_ref, *, t_new):
    batch = ue_ref.shape[1]
    pooled = []
    for g, w in enumerate(POOL_WINDOWS):
        lanes = slice(g * POOL_GROUP_DIM, (g + 1) * POOL_GROUP_DIM)
        per_t = []
        for t in range(t_new):
            cur = POOL_STATE + t
            s = ue_ref[cur - w + 1, :, lanes]
            for r in range(cur - w + 2, cur + 1):
                s = s + ue_ref[r, :, lanes]
            per_t.append(s / float(w) - ue_ref[cur, :, lanes])
        pooled.append(jnp.concatenate(per_t, axis=0))
    gate = g_ref[...].reshape(t_new * batch, POOL_WIDTH).astype(F32)
    o_ref[...] = _pool_mix(pooled, pw_ref, scale_ref, gate).reshape(t_new, batch, POOL_WIDTH).astype(o_ref.dtype)


def _pool_sample(u_ext_t, gate_t, pool_w, pool_scale):
    t_new = gate_t.shape[0]
    return pl.pallas_call(
        functools.partial(_pool_sample_kernel, t_new=t_new),
        out_shape=jax.ShapeDtypeStruct(gate_t.shape, BF16),
        compiler_params=pltpu.CompilerParams(vmem_limit_bytes=VMEM_LIMIT),
        name="pool_sample",
    )(u_ext_t, gate_t, pool_w, pool_scale)


def _block_reference_rows(b, bs_ref, m, t_idx):
    c = b.shape[0]
    if 2 * m >= SUBLANES:
        parts = [jnp.broadcast_to(bs_ref[p * 2 * m + m - 1:p * 2 * m + m, :], (2 * m, b.shape[1]))
                 for p in range(c // (2 * m))]
        return parts[0] if len(parts) == 1 else jnp.concatenate(parts, axis=0)
    if m == 2:
        t4 = t_idx & 3
        return jnp.where(t4 == 0, pltpu.roll(b, c - 1, 0),
                         jnp.where(t4 == 1, b, jnp.where(t4 == 2, pltpu.roll(b, 1, 0), pltpu.roll(b, 2, 0))))
    assert m == 1
    return jnp.where((t_idx & 1) == 0, b, pltpu.roll(b, 1, 0))


def _gla_kernel(q_ref, k_ref, v_ref, g_ref, la_ref, og_ref, s0_ref, sout_all_ref, o_ref, sout_ref, s_ref, bs_ref,
                *, chunk, group):
    del sout_all_ref
    ci = pl.program_id(1)

    @pl.when(ci == 0)
    def _():
        s_ref[...] = s0_ref[...]

    t_row = lax.broadcasted_iota(jnp.int32, (chunk, chunk), 0)
    t_col = lax.broadcasted_iota(jnp.int32, (chunk, chunk), 1)
    t_idx = lax.broadcasted_iota(jnp.int32, (chunk, GLA_DK), 0)
    tril = jnp.where(t_row >= t_col, 1.0, 0.0).astype(BF16)
    pair_diff = t_row ^ t_col

    def one_head(gi, h):
        rows = slice(gi * chunk, (gi + 1) * chunk)
        kl = slice(h * GLA_DK, (h + 1) * GLA_DK)
        vl = slice(h * GLA_DV, (h + 1) * GLA_DV)
        bs = bs_ref.at[gi * GLA_HEADS + h]
        hi, lo = _split_bf16(la_ref[rows, kl])
        b = _dot(tril, hi) + _dot(tril, lo)
        bs[...] = b
        q = q_ref[rows, kl].astype(F32) * (GLA_DK ** -0.5)
        k = k_ref[rows, kl].astype(F32)
        v = v_ref[rows, vl]
        s = s_ref[gi, h]

        o = _dot((q * jnp.exp(b)).astype(BF16), s.astype(BF16))
        scores = jnp.where(pair_diff == 0, _dot_nt(q.astype(BF16), k.astype(BF16)), 0.0)
        m = chunk // 2
        while m >= 1:
            e = jnp.exp(-jnp.abs(b - _block_reference_rows(b, bs, m, t_idx)))
            second = (t_idx & m) != 0
            qf = jnp.where(second, q * e, 0.0).astype(BF16)
            kf = jnp.where(second, 0.0, k * e).astype(BF16)
            scores = scores + jnp.where(pair_diff < 2 * m, _dot_nt(qf, kf), 0.0)
            m //= 2
        o = o + _dot(scores.astype(BF16), v)

        b_end = b[chunk - 1:chunk, :]
        k_end = (k * jnp.exp(b_end - b)).astype(BF16)
        upd = lax.dot_general(k_end, v, _TN, preferred_element_type=F32)
        decay_col = jnp.broadcast_to(jnp.exp(b_end), (GLA_DK, GLA_DK)).T
        s_ref[gi, h] = s * jnp.concatenate([decay_col] * (GLA_DV // GLA_DK), axis=1) + upd

        on = o * lax.rsqrt(jnp.mean(o * o, axis=-1, keepdims=True) + EPS) * og_ref[...]
        o_ref[rows, vl] = (on * _silu(g_ref[rows, vl].astype(F32))).astype(o_ref.dtype)

    for gi in range(group):
        for h in range(GLA_HEADS):
            one_head(gi, h)

    @pl.when(ci == pl.num_programs(1) - 1)
    def _():
        sout_ref[...] = s_ref[...]


def _gla(q, k, v, g, la, onorm_g, s0_all, s0_layer, s_out_all, layer, chunk, group=1):
    batch = s0_all.shape[1]
    nc = q.shape[0] // (batch * chunk)
    assert batch % group == 0 and (group == 1 or nc == 1)
    tok = lambda w: pl.BlockSpec((group * chunk, w), lambda b, c: (b * nc + c, 0))
    state = lambda row: pl.BlockSpec((None, group, GLA_HEADS, GLA_DK, GLA_DV), lambda b, c: (row, b, 0, 0, 0))
    assert s_out_all.shape[1:] == s0_all.shape[1:]
    return pl.pallas_call(
        functools.partial(_gla_kernel, chunk=chunk, group=group),
        grid=(batch // group, nc),
        in_specs=[tok(GLA_KEY_WIDTH), tok(GLA_KEY_WIDTH), tok(GLA_VAL_WIDTH), tok(GLA_VAL_WIDTH), tok(GLA_KEY_WIDTH),
                  pl.BlockSpec(onorm_g.shape, lambda b, c: (0, 0)), state(s0_layer), pl.BlockSpec(memory_space=pl.ANY)],
        out_specs=[tok(GLA_VAL_WIDTH), state(layer)],
        out_shape=[jax.ShapeDtypeStruct(v.shape, BF16), jax.ShapeDtypeStruct(s_out_all.shape, F32)],
        scratch_shapes=[pltpu.VMEM((group, GLA_HEADS, GLA_DK, GLA_DV), F32),
                        pltpu.VMEM((group * GLA_HEADS, chunk, GLA_DK), F32)],
        input_output_aliases={7: 1},
        compiler_params=_params("parallel", "arbitrary"),
        name="gla",
    )(q, k, v, g, la, onorm_g, s0_all, s_out_all)


def _merge_kernel(x_ref, ng_ref, wg_ref, oa_ref, sbg_ref, ob_ref, oc_ref, wa_ref, wb_ref, wc_ref, wo_ref, y_ref):
    x = x_ref[...]
    ms = jnp.mean(x * x, axis=-1, keepdims=True)
    h = (x * lax.rsqrt(ms + EPS) * ng_ref[...]).astype(BF16)
    a = (oa_ref[...].astype(F32) * _silu(sbg_ref[...].astype(F32))).astype(BF16)
    branches = (_dot(a, wa_ref[...]), _dot(ob_ref[...], wb_ref[...]), _dot(oc_ref[...], wc_ref[...]))
    merged = None
    for n, p in enumerate(branches):
        gate = 1.0 / (1.0 + jnp.exp(-_dot(h, wg_ref[:, n * D_MODEL:(n + 1) * D_MODEL])))
        merged = gate * p if merged is None else merged + gate * p
    y_ref[...] = x + _dot(merged.astype(BF16), wo_ref[...])


def _merge(x, oa, sbg, ob, oc, lw, tm):
    m = x.shape[0]
    assert m % tm == 0
    row = lambda w: pl.BlockSpec((tm, w), lambda i: (i, 0))
    full = lambda a: pl.BlockSpec(a.shape, lambda i: (0,) * a.ndim)
    return pl.pallas_call(
        _merge_kernel,
        grid=(m // tm,),
        in_specs=[row(D_MODEL), full(lw["norm_g"]), full(lw["w_mg"]), row(SB_WIDTH), row(SB_WIDTH), row(POOL_WIDTH),
                  row(GLA_VAL_WIDTH), full(lw["w_pa"]), full(lw["w_pb"]), full(lw["w_pc"]), full(lw["w_o"])],
        out_specs=row(D_MODEL),
        out_shape=jax.ShapeDtypeStruct((m, D_MODEL), F32),
        compiler_params=_params("parallel"),
        name="merge",
    )(x, lw["norm_g"], lw["w_mg"], oa, sbg, ob, oc, lw["w_pa"], lw["w_pb"], lw["w_pc"], lw["w_o"])


def _layer_weights(l, norm_g, w_in, sb_qnorm_g, sb_knorm_g, pool_w, pool_scale, gla_w2, gla_b2, gla_onorm_g,
                   w_pa, w_pb, w_pc, w_o):
    w = w_in[l]
    head = jnp.arange(SB_WIDTH, dtype=jnp.int32) // SB_HEAD_DIM
    return {
        "norm_g": norm_g[l][None, :],
        "w_main": w[:, :N_MAIN].astype(BF16),
        "w_r": jnp.pad(w[:, N_MAIN:N_MAIN + GLA_RANK], ((0, 0), (0, LANES - GLA_RANK))).astype(BF16),
        "w_mg": w[:, N_MAIN + GLA_RANK:].astype(BF16),
        "w2": jnp.pad(gla_w2[l], ((0, LANES - GLA_RANK), (0, 0))).astype(BF16),
        "b2": gla_b2[l][None, :],
        "qg": jnp.tile(sb_qnorm_g[l], SB_HEADS)[None, :],
        "kg": jnp.tile(sb_knorm_g[l], SB_HEADS)[None, :],
        "head_mean": jnp.where(head[:, None] == head[None, :], 1.0 / SB_HEAD_DIM, 0.0).astype(BF16),
        "pool_w": pool_w[l].astype(BF16),
        "pool_scale": pool_scale[l][None, :],
        "onorm_g": gla_onorm_g[l][None, :],
        "w_pa": w_pa[l].astype(BF16),
        "w_pb": w_pb[l].astype(BF16),
        "w_pc": w_pc[l].astype(BF16),
        "w_o": w_o[l].astype(BF16),
    }


def _new_outputs(depth, n_seq, seq, batch):
    kv = (depth, n_seq, SB_WIDTH, seq)
    return jnp.zeros(kv, F32), jnp.zeros(kv, F32), jnp.zeros((depth, batch, GLA_HEADS, GLA_DK, GLA_DV), F32)


def _layer_prompt(x, lw, sb_bias_l, batch, seq, layer, carried, tm=256, gla_chunk=128):
    p = _proj(x, lw, tm, seq, layer, carried[:2])
    oa = _attn_prompt(p["q"], p["kb"], p["vb"], sb_bias_l * LOG2E, batch, seq)
    ob = _pool_prompt(p["plu"], p["plg"], lw["pool_w"], lw["pool_scale"], seq, tm)
    s0 = jnp.zeros((1, batch, GLA_HEADS, GLA_DK, GLA_DV), F32)
    oc, s_out = _gla(p["glq"], p["glk"], p["glv"], p["glg"], p["la"], lw["onorm_g"], s0, 0, carried[2], layer,
                     gla_chunk)
    y = _merge(x, oa, p["sbg"], ob, oc, lw, tm)
    pool_state = p["plu"].reshape(batch, seq, POOL_WIDTH)[:, seq - POOL_STATE:]
    return y, (p["kt"], p["vt"], s_out), pool_state


def _layer_sample(x, lw, sb_bias_l, cache_k, cache_v, page_table, pool_prev, gla_all, base, batch, t_new, layer,
                  carried, tm, gla_group=4):
    t_pad = SUBLANES
    page = cache_k.shape[2]
    assert t_new <= t_pad and page_table.shape[0] // batch * page >= POOL_STATE

    def pad_t(a):
        a = a.reshape(batch, t_new, a.shape[-1])
        return jnp.pad(a, ((0, 0), (0, t_pad - t_new), (0, 0))).reshape(batch * t_pad, a.shape[-1])

    def unpad_t(a):
        return a.reshape(batch, t_pad, a.shape[-1])[:, :t_new].reshape(batch * t_new, a.shape[-1])

    p = _proj(x, lw, tm, batch * t_new, layer, carried[:2])
    bias_rows = jnp.broadcast_to(jnp.repeat(sb_bias_l * LOG2E, t_pad)[:, None], (SB_HEADS * t_pad, page))
    oa = unpad_t(_attn_sample(pad_t(p["q"]), pad_t(p["kb"]), pad_t(p["vb"]), cache_k, cache_v, page_table,
                              bias_rows, base, t_pad))

    u_new = p["plu"].reshape(batch, t_new, POOL_WIDTH)
    u_ext = jnp.concatenate([pool_prev, u_new], axis=1)
    gate_t = p["plg"].reshape(batch, t_new, POOL_WIDTH).transpose(1, 0, 2)
    ob = _pool_sample(u_ext.transpose(1, 0, 2), gate_t, lw["pool_w"], lw["pool_scale"])
    ob = ob.transpose(1, 0, 2).reshape(batch * t_new, POOL_WIDTH)

    group = gla_group if batch % gla_group == 0 else 1
    oc, s_out = _gla(pad_t(p["glq"]), pad_t(p["glk"]), pad_t(p["glv"]), pad_t(p["glg"]), pad_t(p["la"]),
                     lw["onorm_g"], gla_all, layer, carried[2], layer, t_pad, group)
    y = _merge(x, oa, p["sbg"], ob, unpad_t(oc), lw, tm)
    return y, (p["kt"], p["vt"], s_out), u_ext[:, t_new:]


def kernel(x_prompt, x_sample, cache_k, cache_v, state_pool, state_gla, page_table, norm_g, w_in, sb_qnorm_g,
           sb_knorm_g, sb_bias, pool_w, pool_scale, gla_w2, gla_b2, gla_onorm_g, w_pa, w_pb, w_pc, w_o):
    bp, seq, _ = x_prompt.shape
    bs, t_new, _ = x_sample.shape
    depth, n_pool, page = cache_k.shape[:3]
    ck = cache_k.transpose(0, 1, 3, 4, 2).reshape(depth * n_pool, SB_WIDTH, page)
    cv = cache_v.transpose(0, 1, 3, 4, 2).reshape(depth * n_pool, SB_WIDTH, page)
    pt = page_table.reshape(-1).astype(jnp.int32)
    y_p = x_prompt.reshape(bp * seq, D_MODEL)
    y_s = x_sample.reshape(bs * t_new, D_MODEL)
    car_p = _new_outputs(depth, bp, seq, bp)
    car_s = _new_outputs(depth, 1, bs * t_new, bs)
    pools_p, pools_s = [], []
    for l in range(depth):
        lw = _layer_weights(l, norm_g, w_in, sb_qnorm_g, sb_knorm_g, pool_w, pool_scale, gla_w2, gla_b2,
                            gla_onorm_g, w_pa, w_pb, w_pc, w_o)
        y_p, car_p, pool_p = _layer_prompt(y_p, lw, sb_bias[l], bp, seq, l, car_p)
        y_s, car_s, pool_s = _layer_sample(y_s, lw, sb_bias[l], ck, cv, pt, state_pool[l], state_gla, l * n_pool, bs,
                                           t_new, l, car_s, tm=min(256, bs * t_new))
        pools_p.append(pool_p)
        pools_s.append(pool_s)

    def token_major(kt, nb, t):
        return kt.reshape(depth, SB_HEADS, SB_HEAD_DIM, nb, t).transpose(0, 3, 4, 1, 2)

    return (y_p.reshape(bp, seq, D_MODEL), y_s.reshape(bs, t_new, D_MODEL),
            token_major(car_p[0].transpose(0, 2, 1, 3), bp, seq), token_major(car_p[1].transpose(0, 2, 1, 3), bp, seq),
            jnp.stack(pools_p), car_p[2],
            token_major(car_s[0], bs, t_new), token_major(car_s[1], bs, t_new), jnp.stack(pools_s), car_s[2])
```

```python
import functools
import math

import jax
import jax.numpy as jnp
from jax import lax
from jax.experimental import pallas as pl
from jax.experimental.pallas import tpu as pltpu

F32 = jnp.float32
BF16 = jnp.bfloat16

D_MODEL = 1024
SB_HEADS = 8
SB_HEAD_DIM = 64
SB_WIDTH = SB_HEADS * SB_HEAD_DIM
POOL_WINDOWS = (2, 4, 8, 16)
POOL_GROUPS = 4
POOL_WIDTH = 512
POOL_GROUP_DIM = POOL_WIDTH // POOL_GROUPS
POOL_STATE = max(POOL_WINDOWS) - 1
POOL_HALO = 16
GLA_HEADS = 4
GLA_DK = 128
GLA_DV = 256
GLA_KEY_WIDTH = GLA_HEADS * GLA_DK
GLA_VAL_WIDTH = GLA_HEADS * GLA_DV
GLA_RANK = 16
GLA_TAU = 16.0
N_BRANCH = 3
EPS = 1e-6
LOG2E = math.log2(math.e)
LANES = 128
SUBLANES = 8
N_MAIN = 4 * SB_WIDTH + 2 * POOL_WIDTH + 2 * GLA_KEY_WIDTH + 2 * GLA_VAL_WIDTH
VMEM_LIMIT = 48 * 1024 * 1024

_NT = (((1,), (1,)), ((), ()))
_TN = (((0,), (0,)), ((), ()))


def _dot(a, b):
    return jnp.dot(a, b, preferred_element_type=F32)


def _dot_nt(a, b):
    return lax.dot_general(a, b, _NT, preferred_element_type=F32)


def _silu(g):
    return g / (1.0 + jnp.exp(-g))


def _softplus2(z):
    one = jnp.ones((), z.dtype)
    return jnp.maximum(z, jnp.zeros((), z.dtype)) + jnp.log2(one + jnp.exp2(-jnp.abs(z)))


def _split_bf16(x):
    hi = x.astype(BF16)
    lo = (x - hi.astype(F32)).astype(BF16)
    return hi, lo


def _params(*sem):
    return pltpu.CompilerParams(dimension_semantics=sem, vmem_limit_bytes=VMEM_LIMIT)


def _proj_kernel(x_ref, ng_ref, w_ref, wr_ref, w2_ref, b2_ref, qg_ref, kg_ref, hm_ref, kt_all_ref, vt_all_ref,
                 q_ref, kt_ref, vt_ref, kb_ref, vb_ref, sbg_ref, plu_ref, plg_ref,
                 glq_ref, glk_ref, glv_ref, glg_ref, la_ref):
    del kt_all_ref, vt_all_ref
    x = x_ref[...]
    ms = jnp.mean(x * x, axis=-1, keepdims=True)
    h = (x * lax.rsqrt(ms + EPS) * ng_ref[...]).astype(BF16)

    def cols(start, width):
        return _dot(h, w_ref[:, start:start + width])

    def head_norm(y, g):
        m = _dot((y * y).astype(BF16), hm_ref[...])
        return y * lax.rsqrt(m + EPS) * g

    qn = head_norm(cols(0, SB_WIDTH), qg_ref[...])
    q_ref[...] = (qn * (LOG2E * SB_HEAD_DIM ** -0.5)).astype(BF16)
    kn = head_norm(cols(SB_WIDTH, SB_WIDTH), kg_ref[...])
    kt_ref[...] = kn.T
    kb_ref[...] = kn.astype(BF16)
    v = cols(2 * SB_WIDTH, SB_WIDTH)
    vt_ref[...] = v.T
    vb_ref[...] = v.astype(BF16)
    sbg_ref[...] = cols(3 * SB_WIDTH, SB_WIDTH).astype(BF16)
    off = 4 * SB_WIDTH
    plu_ref[...] = cols(off, POOL_WIDTH)
    plg_ref[...] = cols(off + POOL_WIDTH, POOL_WIDTH).astype(BF16)
    off += 2 * POOL_WIDTH
    glq_ref[...] = cols(off, GLA_KEY_WIDTH).astype(BF16)
    glk_ref[...] = cols(off + GLA_KEY_WIDTH, GLA_KEY_WIDTH).astype(BF16)
    off += 2 * GLA_KEY_WIDTH
    glv_ref[...] = cols(off, GLA_VAL_WIDTH).astype(BF16)
    glg_ref[...] = cols(off + GLA_VAL_WIDTH, GLA_VAL_WIDTH).astype(BF16)
    r = _dot(h, wr_ref[...])
    pre = _dot(r.astype(BF16), w2_ref[...]) + b2_ref[...]
    la_ref[...] = (jnp.minimum(pre, 0.0) - jnp.log(1.0 + jnp.exp(-jnp.abs(pre)))) * (1.0 / GLA_TAU)


def _proj(x, lw, tm, seq, layer, kv_all):
    m = x.shape[0]
    assert m % seq == 0 and seq % tm == 0
    per_seq = seq // tm
    row = lambda w: pl.BlockSpec((tm, w), lambda i: (i, 0))
    full = lambda a: pl.BlockSpec(a.shape, lambda i: (0,) * a.ndim)
    feat = pl.BlockSpec((None, None, SB_WIDTH, tm), lambda i: (layer, i // per_seq, 0, i % per_seq))
    consts = (lw["norm_g"], lw["w_main"], lw["w_r"], lw["w2"], lw["b2"], lw["qg"], lw["kg"], lw["head_mean"])
    widths = (SB_WIDTH, None, None, SB_WIDTH, SB_WIDTH, SB_WIDTH, POOL_WIDTH, POOL_WIDTH,
              GLA_KEY_WIDTH, GLA_KEY_WIDTH, GLA_VAL_WIDTH, GLA_VAL_WIDTH, GLA_KEY_WIDTH)
    dtypes = (BF16, F32, F32, BF16, BF16, BF16, F32, BF16, BF16, BF16, BF16, BF16, F32)
    kv_shape = jax.ShapeDtypeStruct(kv_all[0].shape, F32)
    assert kv_all[0].shape[1:] == (m // seq, SB_WIDTH, seq)
    n_in = 1 + len(consts)
    outs = pl.pallas_call(
        _proj_kernel,
        grid=(m // tm,),
        in_specs=[row(D_MODEL)] + [full(a) for a in consts] + [pl.BlockSpec(memory_space=pl.ANY)] * 2,
        out_specs=[feat if w is None else row(w) for w in widths],
        out_shape=[kv_shape if w is None else jax.ShapeDtypeStruct((m, w), d) for w, d in zip(widths, dtypes)],
        input_output_aliases={n_in: 1, n_in + 1: 2},
        compiler_params=_params("parallel"),
        name="proj",
    )(x, *consts, *kv_all)
    names = ("q", "kt", "vt", "kb", "vb", "sbg", "plu", "plg", "glq", "glk", "glv", "glg", "la")
    return dict(zip(names, outs))


def _attn_prompt_kernel(bias_ref, q_ref, k_ref, v_ref, o_ref, *, tq, tk, unrolls, heads):
    g = pl.program_id(1)
    i = pl.program_id(2)
    lane = lax.broadcasted_iota(jnp.int32, (tq, LANES), 1)
    tri = jnp.where(lax.broadcasted_iota(jnp.int32, (tk, tk), 0) > lax.broadcasted_iota(jnp.int32, (tk, tk), 1),
                    1.0, 0.0).astype(BF16)
    kd = (i * tq) // tk
    qhs, biases = [], []
    for hh in range(heads):
        q = q_ref[:, (hh // 2) * LANES:(hh // 2 + 1) * LANES]
        qhs.append(jnp.where((lane >= SB_HEAD_DIM) == (hh % 2 == 1), q, jnp.zeros_like(q)))
        biases.append(bias_ref[heads * g + hh])

    def front(hh, kb, valid):
        ks = k_ref[pl.ds(pl.multiple_of(kb * tk, tk), tk), (hh // 2) * LANES:(hh // 2 + 1) * LANES]
        z = _dot_nt(qhs[hh], ks) + biases[hh]
        sp = _softplus2(z)
        if valid is not None:
            sp = jnp.where(valid, sp, 0.0)
        later = _dot(sp.astype(BF16), tri)
        return (z - sp) - later, later[:, 0:1] + sp[:, 0:1]

    def back(hh, pre, tot, kb, valid, acc, c):
        vs = v_ref[pl.ds(pl.multiple_of(kb * tk, tk), tk), (hh // 2) * LANES:(hh // 2 + 1) * LANES]
        a = jnp.exp2(pre - c)
        if valid is not None:
            a = jnp.where(valid, a, 0.0)
        return acc + _dot(a.astype(BF16), vs), c + tot

    def blocks(kbs, valid, carry):
        fronts = [[front(hh, kb, valid) for hh in range(heads)] for kb in kbs]
        carry = list(carry)
        for kb, fr in zip(kbs, fronts):
            for hh in range(heads):
                carry[hh] = back(hh, *fr[hh], kb, valid, *carry[hh])
        return tuple(carry)

    q_pos = i * tq + lax.broadcasted_iota(jnp.int32, (tq, tk), 0)
    k_pos = kd * tk + lax.broadcasted_iota(jnp.int32, (tq, tk), 1)
    zero = (jnp.zeros((tq, LANES), F32), jnp.zeros((tq, 1), F32))
    carry = blocks([kd], k_pos < q_pos, (zero,) * heads)
    left = kd
    for u in unrolls:
        top = left
        carry = lax.fori_loop(
            0, left // u, lambda j, cr: blocks([top - 1 - j * u - r for r in range(u)], None, cr), carry)
        left = left % u
    for t in range(heads // 2):
        o_ref[:, t * LANES:(t + 1) * LANES] = jnp.where(
            lane < SB_HEAD_DIM, carry[2 * t][0], carry[2 * t + 1][0]).astype(o_ref.dtype)


def _attn_prompt(q, k, v, bias, batch, seq, tq=256, tk=256, unrolls=(4, 1), heads=4):
    assert seq % tk == 0 and tk % tq == 0 and unrolls[-1] == 1 and heads % 2 == 0 and SB_HEADS % heads == 0
    nq = seq // tq
    width = heads // 2 * LANES
    return pl.pallas_call(
        functools.partial(_attn_prompt_kernel, tq=tq, tk=tk, unrolls=unrolls, heads=heads),
        grid=(batch, SB_HEADS // heads, nq),
        in_specs=[
            pl.BlockSpec(memory_space=pltpu.SMEM),
            pl.BlockSpec((tq, width), lambda b, g, i: (b * nq + i, g)),
            pl.BlockSpec((seq, width), lambda b, g, i: (b, g)),
            pl.BlockSpec((seq, width), lambda b, g, i: (b, g)),
        ],
        out_specs=pl.BlockSpec((tq, width), lambda b, g, i: (b * nq + i, g)),
        out_shape=jax.ShapeDtypeStruct((batch * seq, SB_WIDTH), BF16),
        compiler_params=_params("parallel", "parallel", "arbitrary"),
        name="attn_prompt",
    )(bias, q, k, v)


def _attn_sample_kernel(pt_ref, bias_ref, q_ref, kn_ref, vn_ref, kc_hbm, vc_hbm, o_ref, kbuf, vbuf, sems,
                        *, page, t_pad, n_pages, base, kblk):
    b = pl.program_id(0)
    nb = pl.num_programs(0)
    slot = b % 2

    def page_copies(seq, slot_):
        copies = []
        for p in range(n_pages):
            idx = base + pt_ref[seq * n_pages + p]
            dst = pl.ds(p * page, page)
            copies.append(pltpu.make_async_copy(kc_hbm.at[idx], kbuf.at[slot_, :, dst], sems.at[slot_, 0, p]))
            copies.append(pltpu.make_async_copy(vc_hbm.at[idx], vbuf.at[slot_, :, dst], sems.at[slot_, 1, p]))
        return copies

    @pl.when(b == 0)
    def _():
        for cp in page_copies(0, 0):
            cp.start()

    @pl.when(b + 1 < nb)
    def _():
        for cp in page_copies(b + 1, 1 - slot):
            cp.start()

    rows = SB_HEADS * t_pad
    row = lax.broadcasted_iota(jnp.int32, (rows, SB_WIDTH), 0)
    lane = lax.broadcasted_iota(jnp.int32, (rows, SB_WIDTH), 1)
    own = (lane // SB_HEAD_DIM) == (row // t_pad)
    q8 = q_ref[...]
    qbd = jnp.where(own, jnp.concatenate([q8] * SB_HEADS, axis=0), jnp.zeros((rows, SB_WIDTH), q8.dtype))

    def strict_tri(n):
        return jnp.where(lax.broadcasted_iota(jnp.int32, (n, n), 0) > lax.broadcasted_iota(jnp.int32, (n, n), 1),
                         1.0, 0.0).astype(BF16)

    pad = jnp.zeros((page - t_pad, SB_WIDTH), BF16)
    key_idx = lax.broadcasted_iota(jnp.int32, (rows, page), 1)
    qry_idx = lax.broadcasted_iota(jnp.int32, (rows, page), 0) % t_pad
    valid = key_idx < qry_idx
    z = _dot_nt(qbd, jnp.concatenate([kn_ref[...], pad], axis=0)) + bias_ref[...]
    sp = jnp.where(valid, _softplus2(z), 0.0)
    later = _dot(sp.astype(BF16), strict_tri(page))
    a = jnp.where(valid, jnp.exp2((z - sp) - later), 0.0)
    acc = _dot(a.astype(BF16), jnp.concatenate([vn_ref[...], pad], axis=0))
    c = later[:, 0:1] + sp[:, 0:1]

    for cp in page_copies(b, slot):
        cp.wait()

    past = n_pages * page
    z = _dot(qbd, kbuf[slot].astype(BF16)) + bias_ref[:, 0:1]
    sp = _softplus2(z)
    tri = strict_tri(kblk)
    a_blocks = [None] * (past // kblk)
    for j in reversed(range(past // kblk)):
        cols = slice(j * kblk, (j + 1) * kblk)
        later = _dot(sp[:, cols].astype(BF16), tri)
        a_blocks[j] = jnp.exp2((z[:, cols] - sp[:, cols]) - later - c).astype(BF16)
        c = c + (later[:, 0:1] + sp[:, j * kblk:j * kblk + 1])
    acc = acc + _dot_nt(jnp.concatenate(a_blocks, axis=1), vbuf[slot].astype(BF16))

    picked = jnp.where(own, acc, 0.0)
    o_ref[...] = picked.reshape(SB_HEADS, t_pad, SB_WIDTH).sum(axis=0).astype(o_ref.dtype)


def _attn_sample(q, kn, vn, cache_kt, cache_vt, page_table, bias_rows, base, t_pad, kblk=256):
    batch = q.shape[0] // t_pad
    n_pages = page_table.shape[0] // batch
    page = cache_kt.shape[2]
    kblk = min(kblk, n_pages * page)
    assert (n_pages * page) % kblk == 0
    new_spec = pl.BlockSpec((t_pad, SB_WIDTH), lambda b, pt: (b, 0))
    hbm = pl.BlockSpec(memory_space=pl.ANY)
    grid_spec = pltpu.PrefetchScalarGridSpec(
        num_scalar_prefetch=1,
        grid=(batch,),
        in_specs=[pl.BlockSpec(bias_rows.shape, lambda b, pt: (0, 0)), new_spec, new_spec, new_spec, hbm, hbm],
        out_specs=new_spec,
        scratch_shapes=[pltpu.VMEM((2, SB_WIDTH, n_pages * page), F32), pltpu.VMEM((2, SB_WIDTH, n_pages * page), F32),
                        pltpu.SemaphoreType.DMA((2, 2, n_pages))],
    )
    return pl.pallas_call(
        functools.partial(_attn_sample_kernel, page=page, t_pad=t_pad, n_pages=n_pages, base=base, kblk=kblk),
        grid_spec=grid_spec,
        out_shape=jax.ShapeDtypeStruct(q.shape, BF16),
        compiler_params=_params("arbitrary"),
        name="attn_sample",
    )(page_table, bias_rows, q, kn, vn, cache_kt, cache_vt)


def _pool_mix(pooled_groups, pw_ref, scale_ref, gate):
    mixed = jnp.concatenate([_dot(p.astype(BF16), pw_ref[g]) for g, p in enumerate(pooled_groups)], axis=1)
    return mixed * scale_ref[...] * _silu(gate)


def _pool_prompt_kernel(u_ref, halo_ref, g_ref, pw_ref, scale_ref, o_ref, *, tm, seq):
    i = pl.program_id(0)
    start = (i * tm) % seq
    u = u_ref[...]
    halo = jnp.where(start == 0, 0.0, halo_ref[...])
    ext = jnp.concatenate([halo, u], axis=0)
    pos = start + lax.broadcasted_iota(jnp.int32, (tm, 1), 0)
    pooled = []
    for g, w in enumerate(POOL_WINDOWS):
        s = ext[:, g * POOL_GROUP_DIM:(g + 1) * POOL_GROUP_DIM]
        k = 1
        while k < w:
            s = s + pltpu.roll(s, k, 0)
            k *= 2
        count = jnp.minimum(w, pos + 1).astype(F32)
        pooled.append(s[POOL_HALO:] / count - u[:, g * POOL_GROUP_DIM:(g + 1) * POOL_GROUP_DIM])
    o_ref[...] = _pool_mix(pooled, pw_ref, scale_ref, g_ref[...].astype(F32)).astype(o_ref.dtype)


def _pool_prompt(u, gate, pool_w, pool_scale, seq, tm):
    m = u.shape[0]
    assert seq % tm == 0 and tm % POOL_HALO == 0
    ratio = tm // POOL_HALO
    return pl.pallas_call(
        functools.partial(_pool_prompt_kernel, tm=tm, seq=seq),
        grid=(m // tm,),
        in_specs=[
            pl.BlockSpec((tm, POOL_WIDTH), lambda i: (i, 0)),
            pl.BlockSpec((POOL_HALO, POOL_WIDTH), lambda i: (jnp.maximum(i * ratio - 1, 0), 0)),
            pl.BlockSpec((tm, POOL_WIDTH), lambda i: (i, 0)),
            pl.BlockSpec(pool_w.shape, lambda i: (0, 0, 0)),
            pl.BlockSpec(pool_scale.shape, lambda i: (0, 0)),
        ],
        out_specs=pl.BlockSpec((tm, POOL_WIDTH), lambda i: (i, 0)),
        out_shape=jax.ShapeDtypeStruct((m, POOL_WIDTH), BF16),
        compiler_params=_params("parallel"),
        name="pool_prompt",
    )(u, u, gate, pool_w, pool_scale)


def _pool_sample_kernel(ue_ref, g_ref, pw_ref, scale_ref, o_ref, *, t_new):
    batch = ue_ref.shape[1]
    pooled = []
    for g, w in enumerate(POOL_WINDOWS):
        lanes = slice(g * POOL_GROUP_DIM, (g + 1) * POOL_GROUP_DIM)
        per_t = []
        for t in range(t_new):
            cur = POOL_STATE + t
            s = ue_ref[cur - w + 1, :, lanes]
            for r in range(cur - w + 2, cur + 1):
                s = s + ue_ref[r, :, lanes]
            per_t.append(s / float(w) - ue_ref[cur, :, lanes])
        pooled.append(jnp.concatenate(per_t, axis=0))
    gate = g_ref[...].reshape(t_new * batch, POOL_WIDTH).astype(F32)
    o_ref[...] = _pool_mix(pooled, pw_ref, scale_ref, gate).reshape(t_new, batch, POOL_WIDTH).astype(o_ref.dtype)


def _pool_sample(u_ext_t, gate_t, pool_w, pool_scale):
    t_new = gate_t.shape[0]
    return pl.pallas_call(
        functools.partial(_pool_sample_kernel, t_new=t_new),
        out_shape=jax.ShapeDtypeStruct(gate_t.shape, BF16),
        compiler_params=pltpu.CompilerParams(vmem_limit_bytes=VMEM_LIMIT),
        name="pool_sample",
    )(u_ext_t, gate_t, pool_w, pool_scale)


def _block_reference_rows(b, bs_ref, m, t_idx):
    c = b.shape[0]
    if 2 * m >= SUBLANES:
        parts = [jnp.broadcast_to(bs_ref[p * 2 * m + m - 1:p * 2 * m + m, :], (2 * m, b.shape[1]))
                 for p in range(c // (2 * m))]
        return parts[0] if len(parts) == 1 else jnp.concatenate(parts, axis=0)
    if m == 2:
        t4 = t_idx & 3
        return jnp.where(t4 == 0, pltpu.roll(b, c - 1, 0),
                         jnp.where(t4 == 1, b, jnp.where(t4 == 2, pltpu.roll(b, 1, 0), pltpu.roll(b, 2, 0))))
    assert m == 1
    return jnp.where((t_idx & 1) == 0, b, pltpu.roll(b, 1, 0))


def _gla_kernel(q_ref, k_ref, v_ref, g_ref, la_ref, og_ref, s0_ref, sout_all_ref, o_ref, sout_ref, s_ref, *bs_refs,
                chunk, group):
    del sout_all_ref
    ci = pl.program_id(1)

    @pl.when(ci == 0)
    def _():
        s_ref[...] = s0_ref[...]

    t_row = lax.broadcasted_iota(jnp.int32, (chunk, chunk), 0)
    t_col = lax.broadcasted_iota(jnp.int32, (chunk, chunk), 1)
    t_idx = lax.broadcasted_iota(jnp.int32, (chunk, GLA_DK), 0)
    tril = jnp.where(t_row >= t_col, 1.0, 0.0).astype(BF16)
    pair_diff = t_row ^ t_col

    def one_head(gi, h, s):
        rows = slice(gi * chunk, (gi + 1) * chunk)
        kl = slice(h * GLA_DK, (h + 1) * GLA_DK)
        vl = slice(h * GLA_DV, (h + 1) * GLA_DV)
        bs = bs_refs[gi * GLA_HEADS + h]
        hi, lo = _split_bf16(la_ref[rows, kl])
        b = _dot(tril, hi) + _dot(tril, lo)
        yield
        bs[...] = b
        q = q_ref[rows, kl].astype(F32) * (GLA_DK ** -0.5)
        k = k_ref[rows, kl].astype(F32)
        v = v_ref[rows, vl]

        o_inter = _dot((q * jnp.exp(b)).astype(BF16), s.astype(BF16))
        terms = [(pair_diff == 0, _dot_nt(q.astype(BF16), k.astype(BF16)))]
        m = chunk // 2
        while m >= 1:
            e = jnp.exp(-jnp.abs(b - _block_reference_rows(b, bs, m, t_idx)))
            second = (t_idx & m) != 0
            qf = jnp.where(second, q * e, 0.0).astype(BF16)
            kf = jnp.where(second, 0.0, k * e).astype(BF16)
            terms.append((pair_diff < 2 * m, _dot_nt(qf, kf)))
            m //= 2
        b_end = b[chunk - 1:chunk, :]
        k_end = (k * jnp.exp(b_end - b)).astype(BF16)
        upd = lax.dot_general(k_end, v, _TN, preferred_element_type=F32)
        yield
        scores = None
        for keep, term in terms:
            part = jnp.where(keep, term, 0.0)
            scores = part if scores is None else scores + part
        o_intra = _dot(scores.astype(BF16), v)
        decay_col = jnp.broadcast_to(jnp.exp(b_end), (GLA_DK, GLA_DK)).T
        s_new = s * jnp.concatenate([decay_col] * (GLA_DV // GLA_DK), axis=1) + upd
        yield
        o = o_inter + o_intra
        on = o * lax.rsqrt(jnp.mean(o * o, axis=-1, keepdims=True) + EPS) * og_ref[...]
        return s_new, (on * _silu(g_ref[rows, vl].astype(F32))).astype(o_ref.dtype)

    pairs = [(gi, h) for gi in range(group) for h in range(GLA_HEADS)]
    chains = [one_head(gi, h, s_ref[gi, h]) for gi, h in pairs]
    results = [None] * len(chains)
    while any(r is None for r in results):
        for n, chain in enumerate(chains):
            if results[n] is None:
                try:
                    next(chain)
                except StopIteration as done:
                    results[n] = done.value
    for (gi, h), (s_new, out) in zip(pairs, results):
        s_ref[gi, h] = s_new
        o_ref[gi * chunk:(gi + 1) * chunk, h * GLA_DV:(h + 1) * GLA_DV] = out

    @pl.when(ci == pl.num_programs(1) - 1)
    def _():
        sout_ref[...] = s_ref[...]


def _gla(q, k, v, g, la, onorm_g, s0_all, s0_layer, s_out_all, layer, chunk, group=1):
    batch = s0_all.shape[1]
    nc = q.shape[0] // (batch * chunk)
    assert batch % group == 0 and (group == 1 or nc == 1)
    tok = lambda w: pl.BlockSpec((group * chunk, w), lambda b, c: (b * nc + c, 0))
    state = lambda row: pl.BlockSpec((None, group, GLA_HEADS, GLA_DK, GLA_DV), lambda b, c: (row, b, 0, 0, 0))
    assert s_out_all.shape[1:] == s0_all.shape[1:]
    return pl.pallas_call(
        functools.partial(_gla_kernel, chunk=chunk, group=group),
        grid=(batch // group, nc),
        in_specs=[tok(GLA_KEY_WIDTH), tok(GLA_KEY_WIDTH), tok(GLA_VAL_WIDTH), tok(GLA_VAL_WIDTH), tok(GLA_KEY_WIDTH),
                  pl.BlockSpec(onorm_g.shape, lambda b, c: (0, 0)), state(s0_layer), pl.BlockSpec(memory_space=pl.ANY)],
        out_specs=[tok(GLA_VAL_WIDTH), state(layer)],
        out_shape=[jax.ShapeDtypeStruct(v.shape, BF16), jax.ShapeDtypeStruct(s_out_all.shape, F32)],
        scratch_shapes=[pltpu.VMEM((group, GLA_HEADS, GLA_DK, GLA_DV), F32)]
        + [pltpu.VMEM((chunk, GLA_DK), F32)] * (group * GLA_HEADS),
        input_output_aliases={7: 1},
        compiler_params=_params("parallel", "arbitrary"),
        name="gla",
    )(q, k, v, g, la, onorm_g, s0_all, s_out_all)


def _merge_kernel(x_ref, ng_ref, wg_ref, oa_ref, sbg_ref, ob_ref, oc_ref, wa_ref, wb_ref, wc_ref, wo_ref, y_ref):
    x = x_ref[...]
    ms = jnp.mean(x * x, axis=-1, keepdims=True)
    h = (x * lax.rsqrt(ms + EPS) * ng_ref[...]).astype(BF16)
    a = (oa_ref[...].astype(F32) * _silu(sbg_ref[...].astype(F32))).astype(BF16)
    branches = (_dot(a, wa_ref[...]), _dot(ob_ref[...], wb_ref[...]), _dot(oc_ref[...], wc_ref[...]))
    merged = None
    for n, p in enumerate(branches):
        gate = 1.0 / (1.0 + jnp.exp(-_dot(h, wg_ref[:, n * D_MODEL:(n + 1) * D_MODEL])))
        merged = gate * p if merged is None else merged + gate * p
    y_ref[...] = x + _dot(merged.astype(BF16), wo_ref[...])


def _merge(x, oa, sbg, ob, oc, lw, tm):
    m = x.shape[0]
    assert m % tm == 0
    row = lambda w: pl.BlockSpec((tm, w), lambda i: (i, 0))
    full = lambda a: pl.BlockSpec(a.shape, lambda i: (0,) * a.ndim)
    return pl.pallas_call(
        _merge_kernel,
        grid=(m // tm,),
        in_specs=[row(D_MODEL), full(lw["norm_g"]), full(lw["w_mg"]), row(SB_WIDTH), row(SB_WIDTH), row(POOL_WIDTH),
                  row(GLA_VAL_WIDTH), full(lw["w_pa"]), full(lw["w_pb"]), full(lw["w_pc"]), full(lw["w_o"])],
        out_specs=row(D_MODEL),
        out_shape=jax.ShapeDtypeStruct((m, D_MODEL), F32),
        compiler_params=_params("parallel"),
        name="merge",
    )(x, lw["norm_g"], lw["w_mg"], oa, sbg, ob, oc, lw["w_pa"], lw["w_pb"], lw["w_pc"], lw["w_o"])


def _layer_weights(l, norm_g, w_in, sb_qnorm_g, sb_knorm_g, pool_w, pool_scale, gla_w2, gla_b2, gla_onorm_g,
                   w_pa, w_pb, w_pc, w_o):
    w = w_in[l]
    head = jnp.arange(SB_WIDTH, dtype=jnp.int32) // SB_HEAD_DIM
    return {
        "norm_g": norm_g[l][None, :],
        "w_main": w[:, :N_MAIN].astype(BF16),
        "w_r": jnp.pad(w[:, N_MAIN:N_MAIN + GLA_RANK], ((0, 0), (0, LANES - GLA_RANK))).astype(BF16),
        "w_mg": w[:, N_MAIN + GLA_RANK:].astype(BF16),
        "w2": jnp.pad(gla_w2[l], ((0, LANES - GLA_RANK), (0, 0))).astype(BF16),
        "b2": gla_b2[l][None, :],
        "qg": jnp.tile(sb_qnorm_g[l], SB_HEADS)[None, :],
        "kg": jnp.tile(sb_knorm_g[l], SB_HEADS)[None, :],
        "head_mean": jnp.where(head[:, None] == head[None, :], 1.0 / SB_HEAD_DIM, 0.0).astype(BF16),
        "pool_w": pool_w[l].astype(BF16),
        "pool_scale": pool_scale[l][None, :],
        "onorm_g": gla_onorm_g[l][None, :],
        "w_pa": w_pa[l].astype(BF16),
        "w_pb": w_pb[l].astype(BF16),
        "w_pc": w_pc[l].astype(BF16),
        "w_o": w_o[l].astype(BF16),
    }


def _new_outputs(depth, n_seq, seq, batch):
    kv = (depth, n_seq, SB_WIDTH, seq)
    return jnp.zeros(kv, F32), jnp.zeros(kv, F32), jnp.zeros((depth, batch, GLA_HEADS, GLA_DK, GLA_DV), F32)


def _layer_prompt(x, lw, sb_bias_l, batch, seq, layer, carried, tm=256, gla_chunk=128):
    p = _proj(x, lw, tm, seq, layer, carried[:2])
    oa = _attn_prompt(p["q"], p["kb"], p["vb"], sb_bias_l * LOG2E, batch, seq)
    ob = _pool_prompt(p["plu"], p["plg"], lw["pool_w"], lw["pool_scale"], seq, tm)
    s0 = jnp.zeros((1, batch, GLA_HEADS, GLA_DK, GLA_DV), F32)
    oc, s_out = _gla(p["glq"], p["glk"], p["glv"], p["glg"], p["la"], lw["onorm_g"], s0, 0, carried[2], layer,
                     gla_chunk)
    y = _merge(x, oa, p["sbg"], ob, oc, lw, tm)
    pool_state = p["plu"].reshape(batch, seq, POOL_WIDTH)[:, seq - POOL_STATE:]
    return y, (p["kt"], p["vt"], s_out), pool_state


def _layer_sample(x, lw, sb_bias_l, cache_k, cache_v, page_table, pool_prev, gla_all, base, batch, t_new, layer,
                  carried, tm, gla_group=4):
    t_pad = SUBLANES
    page = cache_k.shape[2]
    assert t_new <= t_pad and page_table.shape[0] // batch * page >= POOL_STATE

    def pad_t(a):
        a = a.reshape(batch, t_new, a.shape[-1])
        return jnp.pad(a, ((0, 0), (0, t_pad - t_new), (0, 0))).reshape(batch * t_pad, a.shape[-1])

    def unpad_t(a):
        return a.reshape(batch, t_pad, a.shape[-1])[:, :t_new].reshape(batch * t_new, a.shape[-1])

    p = _proj(x, lw, tm, batch * t_new, layer, carried[:2])
    bias_rows = jnp.broadcast_to(jnp.repeat(sb_bias_l * LOG2E, t_pad)[:, None], (SB_HEADS * t_pad, page))
    oa = unpad_t(_attn_sample(pad_t(p["q"]), pad_t(p["kb"]), pad_t(p["vb"]), cache_k, cache_v, page_table,
                              bias_rows, base, t_pad))

    u_new = p["plu"].reshape(batch, t_new, POOL_WIDTH)
    u_ext = jnp.concatenate([pool_prev, u_new], axis=1)
    gate_t = p["plg"].reshape(batch, t_new, POOL_WIDTH).transpose(1, 0, 2)
    ob = _pool_sample(u_ext.transpose(1, 0, 2), gate_t, lw["pool_w"], lw["pool_scale"])
    ob = ob.transpose(1, 0, 2).reshape(batch * t_new, POOL_WIDTH)

    group = gla_group if batch % gla_group == 0 else 1
    oc, s_out = _gla(pad_t(p["glq"]), pad_t(p["glk"]), pad_t(p["glv"]), pad_t(p["glg"]), pad_t(p["la"]),
                     lw["onorm_g"], gla_all, layer, carried[2], layer, t_pad, group)
    y = _merge(x, oa, p["sbg"], ob, unpad_t(oc), lw, tm)
    return y, (p["kt"], p["vt"], s_out), u_ext[:, t_new:]


def kernel(x_prompt, x_sample, cache_k, cache_v, state_pool, state_gla, page_table, norm_g, w_in, sb_qnorm_g,
           sb_knorm_g, sb_bias, pool_w, pool_scale, gla_w2, gla_b2, gla_onorm_g, w_pa, w_pb, w_pc, w_o):
    bp, seq, _ = x_prompt.shape
    bs, t_new, _ = x_sample.shape
    depth, n_pool, page = cache_k.shape[:3]
    ck = cache_k.transpose(0, 1, 3, 4, 2).reshape(depth * n_pool, SB_WIDTH, page)
    cv = cache_v.transpose(0, 1, 3, 4, 2).reshape(depth * n_pool, SB_WIDTH, page)
    pt = page_table.reshape(-1).astype(jnp.int32)
    y_p = x_prompt.reshape(bp * seq, D_MODEL)
    y_s = x_sample.reshape(bs * t_new, D_MODEL)
    car_p = _new_outputs(depth, bp, seq, bp)
    car_s = _new_outputs(depth, 1, bs * t_new, bs)
    pools_p, pools_s = [], []
    for l in range(depth):
        lw = _layer_weights(l, norm_g, w_in, sb_qnorm_g, sb_knorm_g, pool_w, pool_scale, gla_w2, gla_b2,
                            gla_onorm_g, w_pa, w_pb, w_pc, w_o)
        y_p, car_p, pool_p = _layer_prompt(y_p, lw, sb_bias[l], bp, seq, l, car_p)
        y_s, car_s, pool_s = _layer_sample(y_s, lw, sb_bias[l], ck, cv, pt, state_pool[l], state_gla, l * n_pool, bs,
                                           t_new, l, car_s, tm=min(256, bs * t_new))
        pools_p.append(pool_p)
        pools_s.append(pool_s)

    def token_major(kt, nb, t):
        return kt.reshape(depth, SB_HEADS, SB_HEAD_DIM, nb, t).transpose(0, 3, 4, 1, 2)

    return (y_p.reshape(bp, seq, D_MODEL), y_s.reshape(bs, t_new, D_MODEL),
            token_major(car_p[0].transpose(0, 2, 1, 3), bp, seq), token_major(car_p[1].transpose(0, 2, 1, 3), bp, seq),
            jnp.stack(pools_p), car_p[2],
            token_major(car_s[0], bs, t_new), token_major(car_s[1], bs, t_new), jnp.stack(pools_s), car_s[2])
```

```python
import functools
import math

import jax
import jax.numpy as jnp
from jax import lax
from jax.experimental import pallas as pl
from jax.experimental.pallas import tpu as pltpu

F32 = jnp.float32
BF16 = jnp.bfloat16

D_MODEL = 1024
SB_HEADS = 8
SB_HEAD_DIM = 64
SB_WIDTH = SB_HEADS * SB_HEAD_DIM
POOL_WINDOWS = (2, 4, 8, 16)
POOL_GROUPS = 4
POOL_WIDTH = 512
POOL_GROUP_DIM = POOL_WIDTH // POOL_GROUPS
POOL_STATE = max(POOL_WINDOWS) - 1
POOL_HALO = 16
GLA_HEADS = 4
GLA_DK = 128
GLA_DV = 256
GLA_KEY_WIDTH = GLA_HEADS * GLA_DK
GLA_VAL_WIDTH = GLA_HEADS * GLA_DV
GLA_RANK = 16
GLA_TAU = 16.0
N_BRANCH = 3
EPS = 1e-6
LOG2E = math.log2(math.e)
LANES = 128
SUBLANES = 8
N_MAIN = 4 * SB_WIDTH + 2 * POOL_WIDTH + 2 * GLA_KEY_WIDTH + 2 * GLA_VAL_WIDTH
VMEM_LIMIT = 48 * 1024 * 1024

_NT = (((1,), (1,)), ((), ()))
_TN = (((0,), (0,)), ((), ()))


def _dot(a, b):
    return jnp.dot(a, b, preferred_element_type=F32)


def _dot_nt(a, b):
    return lax.dot_general(a, b, _NT, preferred_element_type=F32)


def _silu(g):
    return g / (1.0 + jnp.exp(-g))


def _softplus2(z):
    one = jnp.ones((), z.dtype)
    return jnp.maximum(z, jnp.zeros((), z.dtype)) + jnp.log2(one + jnp.exp2(-jnp.abs(z)))


def _split_bf16(x):
    hi = x.astype(BF16)
    lo = (x - hi.astype(F32)).astype(BF16)
    return hi, lo


def _params(*sem):
    return pltpu.CompilerParams(dimension_semantics=sem, vmem_limit_bytes=VMEM_LIMIT)


def _proj_kernel(x_ref, ng_ref, w_ref, wr_ref, w2_ref, b2_ref, qg_ref, kg_ref, hm_ref, kt_all_ref, vt_all_ref,
                 q_ref, kt_ref, vt_ref, kb_ref, vb_ref, sbg_ref, plu_ref, plg_ref,
                 glq_ref, glk_ref, glv_ref, glg_ref, la_ref):
    del kt_all_ref, vt_all_ref
    x = x_ref[...]
    ms = jnp.mean(x * x, axis=-1, keepdims=True)
    h = (x * lax.rsqrt(ms + EPS) * ng_ref[...]).astype(BF16)

    def cols(start, width):
        return _dot(h, w_ref[:, start:start + width])

    def head_norm(y, g):
        y2 = (y * y).astype(BF16)
        half = SB_WIDTH // 2
        m = jnp.concatenate([_dot(y2[:, :half], hm_ref[:half, :half]), _dot(y2[:, half:], hm_ref[half:, half:])],
                            axis=1)
        return y * lax.rsqrt(m + EPS) * g

    qn = head_norm(cols(0, SB_WIDTH), qg_ref[...])
    q_ref[...] = (qn * (LOG2E * SB_HEAD_DIM ** -0.5)).astype(BF16)
    kn = head_norm(cols(SB_WIDTH, SB_WIDTH), kg_ref[...])
    kt_ref[...] = kn.T
    kb_ref[...] = kn.astype(BF16)
    v = cols(2 * SB_WIDTH, SB_WIDTH)
    vt_ref[...] = v.T
    vb_ref[...] = v.astype(BF16)
    sbg_ref[...] = cols(3 * SB_WIDTH, SB_WIDTH).astype(BF16)
    off = 4 * SB_WIDTH
    plu_ref[...] = cols(off, POOL_WIDTH)
    plg_ref[...] = cols(off + POOL_WIDTH, POOL_WIDTH).astype(BF16)
    off += 2 * POOL_WIDTH
    glq_ref[...] = cols(off, GLA_KEY_WIDTH).astype(BF16)
    glk_ref[...] = cols(off + GLA_KEY_WIDTH, GLA_KEY_WIDTH).astype(BF16)
    off += 2 * GLA_KEY_WIDTH
    glv_ref[...] = cols(off, GLA_VAL_WIDTH).astype(BF16)
    glg_ref[...] = cols(off + GLA_VAL_WIDTH, GLA_VAL_WIDTH).astype(BF16)
    r = _dot(h, wr_ref[...])
    pre = _dot(r.astype(BF16), w2_ref[...]) + b2_ref[...]
    la_ref[...] = (jnp.minimum(pre, 0.0) - jnp.log(1.0 + jnp.exp(-jnp.abs(pre)))) * (1.0 / GLA_TAU)


def _proj(x, lw, tm, seq, layer, kv_all):
    m = x.shape[0]
    assert m % seq == 0 and seq % tm == 0
    per_seq = seq // tm
    row = lambda w: pl.BlockSpec((tm, w), lambda i: (i, 0))
    full = lambda a: pl.BlockSpec(a.shape, lambda i: (0,) * a.ndim)
    feat = pl.BlockSpec((None, None, SB_WIDTH, tm), lambda i: (layer, i // per_seq, 0, i % per_seq))
    consts = (lw["norm_g"], lw["w_main"], lw["w_r"], lw["w2"], lw["b2"], lw["qg"], lw["kg"], lw["head_mean"])
    widths = (SB_WIDTH, None, None, SB_WIDTH, SB_WIDTH, SB_WIDTH, POOL_WIDTH, POOL_WIDTH,
              GLA_KEY_WIDTH, GLA_KEY_WIDTH, GLA_VAL_WIDTH, GLA_VAL_WIDTH, GLA_KEY_WIDTH)
    dtypes = (BF16, F32, F32, BF16, BF16, BF16, F32, BF16, BF16, BF16, BF16, BF16, F32)
    kv_shape = jax.ShapeDtypeStruct(kv_all[0].shape, F32)
    assert kv_all[0].shape[1:] == (m // seq, SB_WIDTH, seq)
    n_in = 1 + len(consts)
    outs = pl.pallas_call(
        _proj_kernel,
        grid=(m // tm,),
        in_specs=[row(D_MODEL)] + [full(a) for a in consts] + [pl.BlockSpec(memory_space=pl.ANY)] * 2,
        out_specs=[feat if w is None else row(w) for w in widths],
        out_shape=[kv_shape if w is None else jax.ShapeDtypeStruct((m, w), d) for w, d in zip(widths, dtypes)],
        input_output_aliases={n_in: 1, n_in + 1: 2},
        compiler_params=_params("parallel"),
        name="proj",
    )(x, *consts, *kv_all)
    names = ("q", "kt", "vt", "kb", "vb", "sbg", "plu", "plg", "glq", "glk", "glv", "glg", "la")
    return dict(zip(names, outs))


def _attn_prompt_kernel(bias_ref, q_ref, k_ref, v_ref, o_ref, *, tq, tk, unrolls, heads):
    g = pl.program_id(1)
    i = pl.program_id(2)
    lane = lax.broadcasted_iota(jnp.int32, (tq, LANES), 1)
    tri = jnp.where(lax.broadcasted_iota(jnp.int32, (tk, tk), 0) > lax.broadcasted_iota(jnp.int32, (tk, tk), 1),
                    1.0, 0.0).astype(BF16)
    kd = (i * tq) // tk
    qhs, biases = [], []
    for hh in range(heads):
        q = q_ref[:, (hh // 2) * LANES:(hh // 2 + 1) * LANES]
        qhs.append(jnp.where((lane >= SB_HEAD_DIM) == (hh % 2 == 1), q, jnp.zeros_like(q)))
        biases.append(bias_ref[heads * g + hh])

    def front(hh, kb, valid):
        ks = k_ref[pl.ds(pl.multiple_of(kb * tk, tk), tk), (hh // 2) * LANES:(hh // 2 + 1) * LANES]
        z = _dot_nt(qhs[hh], ks) + biases[hh]
        sp = _softplus2(z)
        if valid is not None:
            sp = jnp.where(valid, sp, 0.0)
        later = _dot(sp.astype(BF16), tri)
        return (z - sp) - later, later[:, 0:1] + sp[:, 0:1]

    def back(hh, pre, tot, kb, valid, acc, c):
        vs = v_ref[pl.ds(pl.multiple_of(kb * tk, tk), tk), (hh // 2) * LANES:(hh // 2 + 1) * LANES]
        a = jnp.exp2(pre - c)
        if valid is not None:
            a = jnp.where(valid, a, 0.0)
        return acc + _dot(a.astype(BF16), vs), c + tot

    def blocks(kbs, valid, carry):
        fronts = [[front(hh, kb, valid) for hh in range(heads)] for kb in kbs]
        carry = list(carry)
        for kb, fr in zip(kbs, fronts):
            for hh in range(heads):
                carry[hh] = back(hh, *fr[hh], kb, valid, *carry[hh])
        return tuple(carry)

    q_pos = i * tq + lax.broadcasted_iota(jnp.int32, (tq, tk), 0)
    k_pos = kd * tk + lax.broadcasted_iota(jnp.int32, (tq, tk), 1)
    zero = (jnp.zeros((tq, LANES), F32), jnp.zeros((tq, 1), F32))
    carry = blocks([kd], k_pos < q_pos, (zero,) * heads)
    left = kd
    for u in unrolls:
        top = left
        carry = lax.fori_loop(
            0, left // u, lambda j, cr: blocks([top - 1 - j * u - r for r in range(u)], None, cr), carry)
        left = left % u
    for t in range(heads // 2):
        o_ref[:, t * LANES:(t + 1) * LANES] = jnp.where(
            lane < SB_HEAD_DIM, carry[2 * t][0], carry[2 * t + 1][0]).astype(o_ref.dtype)


def _attn_prompt(q, k, v, bias, batch, seq, tq=256, tk=256, unrolls=(4, 1), heads=4):
    assert seq % tk == 0 and tk % tq == 0 and unrolls[-1] == 1 and heads % 2 == 0 and SB_HEADS % heads == 0
    nq = seq // tq
    width = heads // 2 * LANES
    return pl.pallas_call(
        functools.partial(_attn_prompt_kernel, tq=tq, tk=tk, unrolls=unrolls, heads=heads),
        grid=(batch, SB_HEADS // heads, nq),
        in_specs=[
            pl.BlockSpec(memory_space=pltpu.SMEM),
            pl.BlockSpec((tq, width), lambda b, g, i: (b * nq + i, g)),
            pl.BlockSpec((seq, width), lambda b, g, i: (b, g)),
            pl.BlockSpec((seq, width), lambda b, g, i: (b, g)),
        ],
        out_specs=pl.BlockSpec((tq, width), lambda b, g, i: (b * nq + i, g)),
        out_shape=jax.ShapeDtypeStruct((batch * seq, SB_WIDTH), BF16),
        compiler_params=_params("parallel", "parallel", "arbitrary"),
        name="attn_prompt",
    )(bias, q, k, v)


def _attn_sample_kernel(pt_ref, bias_ref, q_ref, kn_ref, vn_ref, kc_hbm, vc_hbm, o_ref, kbuf, vbuf, sems,
                        *, page, t_pad, n_pages, base, kblk):
    b = pl.program_id(0)
    nb = pl.num_programs(0)
    slot = b % 2

    def page_copies(seq, slot_):
        copies = []
        for p in range(n_pages):
            idx = base + pt_ref[seq * n_pages + p]
            dst = pl.ds(p * page, page)
            copies.append(pltpu.make_async_copy(kc_hbm.at[idx], kbuf.at[slot_, :, dst], sems.at[slot_, 0, p]))
            copies.append(pltpu.make_async_copy(vc_hbm.at[idx], vbuf.at[slot_, :, dst], sems.at[slot_, 1, p]))
        return copies

    @pl.when(b == 0)
    def _():
        for cp in page_copies(0, 0):
            cp.start()

    @pl.when(b + 1 < nb)
    def _():
        for cp in page_copies(b + 1, 1 - slot):
            cp.start()

    rows = SB_HEADS * t_pad
    row = lax.broadcasted_iota(jnp.int32, (rows, SB_WIDTH), 0)
    lane = lax.broadcasted_iota(jnp.int32, (rows, SB_WIDTH), 1)
    own = (lane // SB_HEAD_DIM) == (row // t_pad)
    q8 = q_ref[...]
    qbd = jnp.where(own, jnp.concatenate([q8] * SB_HEADS, axis=0), jnp.zeros((rows, SB_WIDTH), q8.dtype))

    def strict_tri(n):
        return jnp.where(lax.broadcasted_iota(jnp.int32, (n, n), 0) > lax.broadcasted_iota(jnp.int32, (n, n), 1),
                         1.0, 0.0).astype(BF16)

    pad = jnp.zeros((page - t_pad, SB_WIDTH), BF16)
    key_idx = lax.broadcasted_iota(jnp.int32, (rows, page), 1)
    qry_idx = lax.broadcasted_iota(jnp.int32, (rows, page), 0) % t_pad
    valid = key_idx < qry_idx
    z = _dot_nt(qbd, jnp.concatenate([kn_ref[...], pad], axis=0)) + bias_ref[...]
    sp = jnp.where(valid, _softplus2(z), 0.0)
    later = _dot(sp.astype(BF16), strict_tri(page))
    a = jnp.where(valid, jnp.exp2((z - sp) - later), 0.0)
    acc = _dot(a.astype(BF16), jnp.concatenate([vn_ref[...], pad], axis=0))
    c = later[:, 0:1] + sp[:, 0:1]

    for cp in page_copies(b, slot):
        cp.wait()

    past = n_pages * page
    z = _dot(qbd, kbuf[slot].astype(BF16)) + bias_ref[:, 0:1]
    sp = _softplus2(z)
    tri = strict_tri(kblk)
    a_blocks = [None] * (past // kblk)
    for j in reversed(range(past // kblk)):
        cols = slice(j * kblk, (j + 1) * kblk)
        later = _dot(sp[:, cols].astype(BF16), tri)
        a_blocks[j] = jnp.exp2((z[:, cols] - sp[:, cols]) - later - c).astype(BF16)
        c = c + (later[:, 0:1] + sp[:, j * kblk:j * kblk + 1])
    acc = acc + _dot_nt(jnp.concatenate(a_blocks, axis=1), vbuf[slot].astype(BF16))

    picked = jnp.where(own, acc, 0.0)
    o_ref[...] = picked.reshape(SB_HEADS, t_pad, SB_WIDTH).sum(axis=0).astype(o_ref.dtype)


def _attn_sample(q, kn, vn, cache_kt, cache_vt, page_table, bias_rows, base, t_pad, kblk=256):
    batch = q.shape[0] // t_pad
    n_pages = page_table.shape[0] // batch
    page = cache_kt.shape[2]
    kblk = min(kblk, n_pages * page)
    assert (n_pages * page) % kblk == 0
    new_spec = pl.BlockSpec((t_pad, SB_WIDTH), lambda b, pt: (b, 0))
    hbm = pl.BlockSpec(memory_space=pl.ANY)
    grid_spec = pltpu.PrefetchScalarGridSpec(
        num_scalar_prefetch=1,
        grid=(batch,),
        in_specs=[pl.BlockSpec(bias_rows.shape, lambda b, pt: (0, 0)), new_spec, new_spec, new_spec, hbm, hbm],
        out_specs=new_spec,
        scratch_shapes=[pltpu.VMEM((2, SB_WIDTH, n_pages * page), F32), pltpu.VMEM((2, SB_WIDTH, n_pages * page), F32),
                        pltpu.SemaphoreType.DMA((2, 2, n_pages))],
    )
    return pl.pallas_call(
        functools.partial(_attn_sample_kernel, page=page, t_pad=t_pad, n_pages=n_pages, base=base, kblk=kblk),
        grid_spec=grid_spec,
        out_shape=jax.ShapeDtypeStruct(q.shape, BF16),
        compiler_params=_params("arbitrary"),
        name="attn_sample",
    )(page_table, bias_rows, q, kn, vn, cache_kt, cache_vt)


def _pool_mix(pooled_groups, pw_ref, scale_ref, gate):
    mixed = jnp.concatenate([_dot(p.astype(BF16), pw_ref[g]) for g, p in enumerate(pooled_groups)], axis=1)
    return mixed * scale_ref[...] * _silu(gate)


def _pool_prompt_block(u_ref, halo_ref, g_ref, pw_ref, scale_ref, *, tm, seq):
    i = pl.program_id(0)
    start = (i * tm) % seq
    u = u_ref[...]
    halo = jnp.where(start == 0, 0.0, halo_ref[...])
    ext = jnp.concatenate([halo, u], axis=0)
    pos = start + lax.broadcasted_iota(jnp.int32, (tm, 1), 0)
    pooled = []
    for g, w in enumerate(POOL_WINDOWS):
        s = ext[:, g * POOL_GROUP_DIM:(g + 1) * POOL_GROUP_DIM]
        k = 1
        while k < w:
            s = s + pltpu.roll(s, k, 0)
            k *= 2
        count = jnp.minimum(w, pos + 1).astype(F32)
        pooled.append(s[POOL_HALO:] / count - u[:, g * POOL_GROUP_DIM:(g + 1) * POOL_GROUP_DIM])
    return _pool_mix(pooled, pw_ref, scale_ref, g_ref[...].astype(F32)).astype(BF16)


def _pool_sample_kernel(ue_ref, g_ref, pw_ref, scale_ref, o_ref, *, t_new):
    batch = ue_ref.shape[1]
    pooled = []
    for g, w in enumerate(POOL_WINDOWS):
        lanes = slice(g * POOL_GROUP_DIM, (g + 1) * POOL_GROUP_DIM)
        per_t = []
        for t in range(t_new):
            cur = POOL_STATE + t
            s = ue_ref[cur - w + 1, :, lanes]
            for r in range(cur - w + 2, cur + 1):
                s = s + ue_ref[r, :, lanes]
            per_t.append(s / float(w) - ue_ref[cur, :, lanes])
        pooled.append(jnp.concatenate(per_t, axis=0))
    gate = g_ref[...].reshape(t_new * batch, POOL_WIDTH).astype(F32)
    o_ref[...] = _pool_mix(pooled, pw_ref, scale_ref, gate).reshape(t_new, batch, POOL_WIDTH).astype(o_ref.dtype)


def _pool_sample(u_ext_t, gate_t, pool_w, pool_scale):
    t_new = gate_t.shape[0]
    return pl.pallas_call(
        functools.partial(_pool_sample_kernel, t_new=t_new),
        out_shape=jax.ShapeDtypeStruct(gate_t.shape, BF16),
        compiler_params=pltpu.CompilerParams(vmem_limit_bytes=VMEM_LIMIT),
        name="pool_sample",
    )(u_ext_t, gate_t, pool_w, pool_scale)


def _block_reference_rows(b, bs_ref, m, t_idx):
    c = b.shape[0]
    if 2 * m >= SUBLANES:
        parts = [jnp.broadcast_to(bs_ref[p * 2 * m + m - 1:p * 2 * m + m, :], (2 * m, b.shape[1]))
                 for p in range(c // (2 * m))]
        return parts[0] if len(parts) == 1 else jnp.concatenate(parts, axis=0)
    if m == 2:
        t4 = t_idx & 3
        return jnp.where(t4 == 0, pltpu.roll(b, c - 1, 0),
                         jnp.where(t4 == 1, b, jnp.where(t4 == 2, pltpu.roll(b, 1, 0), pltpu.roll(b, 2, 0))))
    assert m == 1
    return jnp.where((t_idx & 1) == 0, b, pltpu.roll(b, 1, 0))


def _gla_kernel(q_ref, k_ref, v_ref, g_ref, la_ref, og_ref, s0_ref, sout_all_ref, o_ref, sout_ref, s_ref, *bs_refs,
                chunk, group):
    del sout_all_ref
    ci = pl.program_id(1)

    @pl.when(ci == 0)
    def _():
        s_ref[...] = s0_ref[...]

    t_row = lax.broadcasted_iota(jnp.int32, (chunk, chunk), 0)
    t_col = lax.broadcasted_iota(jnp.int32, (chunk, chunk), 1)
    t_idx = lax.broadcasted_iota(jnp.int32, (chunk, GLA_DK), 0)
    tril = jnp.where(t_row >= t_col, 1.0, 0.0).astype(BF16)
    below = jnp.where(t_row > t_col, t_row ^ t_col, 0)
    level_mask = {}
    m = chunk // 2
    while m >= 1:
        level_mask[m] = (below - m).astype(jnp.uint32) < jnp.uint32(m)
        m //= 2

    def one_head(gi, h, s):
        rows = slice(gi * chunk, (gi + 1) * chunk)
        kl = slice(h * GLA_DK, (h + 1) * GLA_DK)
        vl = slice(h * GLA_DV, (h + 1) * GLA_DV)
        bs = bs_refs[gi * GLA_HEADS + h]
        hi, lo = _split_bf16(la_ref[rows, kl])
        b = _dot(tril, hi) + _dot(tril, lo)
        yield
        bs[...] = b
        q = q_ref[rows, kl].astype(F32) * (GLA_DK ** -0.5)
        k = k_ref[rows, kl].astype(F32)
        v = v_ref[rows, vl]

        o_inter = _dot((q * jnp.exp(b)).astype(BF16), s.astype(BF16))
        terms = [(t_row == t_col, _dot_nt(q.astype(BF16), k.astype(BF16)))]
        m = chunk // 2
        while m >= 1:
            e = jnp.exp(-jnp.abs(b - _block_reference_rows(b, bs, m, t_idx)))
            mixed = (jnp.where((t_idx & m) != 0, q, k) * e).astype(BF16)
            terms.append((level_mask[m], _dot_nt(mixed, mixed)))
            m //= 2
        b_end = b[chunk - 1:chunk, :]
        k_end = (k * jnp.exp(b_end - b)).astype(BF16)
        upd = lax.dot_general(k_end, v, _TN, preferred_element_type=F32)
        yield
        scores = None
        for keep, term in terms:
            part = jnp.where(keep, term, 0.0)
            scores = part if scores is None else scores + part
        o_intra = _dot(scores.astype(BF16), v)
        decay_col = jnp.broadcast_to(jnp.exp(b_end), (GLA_DK, GLA_DK)).T
        s_new = s * jnp.concatenate([decay_col] * (GLA_DV // GLA_DK), axis=1) + upd
        yield
        o = o_inter + o_intra
        on = o * lax.rsqrt(jnp.mean(o * o, axis=-1, keepdims=True) + EPS) * og_ref[...]
        return s_new, (on * _silu(g_ref[rows, vl].astype(F32))).astype(o_ref.dtype)

    pairs = [(gi, h) for gi in range(group) for h in range(GLA_HEADS)]
    chains = [one_head(gi, h, s_ref[gi, h]) for gi, h in pairs]
    results = [None] * len(chains)
    while any(r is None for r in results):
        for n, chain in enumerate(chains):
            if results[n] is None:
                try:
                    next(chain)
                except StopIteration as done:
                    results[n] = done.value
    for (gi, h), (s_new, out) in zip(pairs, results):
        s_ref[gi, h] = s_new
        o_ref[gi * chunk:(gi + 1) * chunk, h * GLA_DV:(h + 1) * GLA_DV] = out

    @pl.when(ci == pl.num_programs(1) - 1)
    def _():
        sout_ref[...] = s_ref[...]


def _gla(q, k, v, g, la, onorm_g, s0_all, s0_layer, s_out_all, layer, chunk, group=1):
    batch = s0_all.shape[1]
    nc = q.shape[0] // (batch * chunk)
    assert batch % group == 0 and (group == 1 or nc == 1)
    tok = lambda w: pl.BlockSpec((group * chunk, w), lambda b, c: (b * nc + c, 0))
    state = lambda row: pl.BlockSpec((None, group, GLA_HEADS, GLA_DK, GLA_DV), lambda b, c: (row, b, 0, 0, 0))
    assert s_out_all.shape[1:] == s0_all.shape[1:]
    return pl.pallas_call(
        functools.partial(_gla_kernel, chunk=chunk, group=group),
        grid=(batch // group, nc),
        in_specs=[tok(GLA_KEY_WIDTH), tok(GLA_KEY_WIDTH), tok(GLA_VAL_WIDTH), tok(GLA_VAL_WIDTH), tok(GLA_KEY_WIDTH),
                  pl.BlockSpec(onorm_g.shape, lambda b, c: (0, 0)), state(s0_layer), pl.BlockSpec(memory_space=pl.ANY)],
        out_specs=[tok(GLA_VAL_WIDTH), state(layer)],
        out_shape=[jax.ShapeDtypeStruct(v.shape, BF16), jax.ShapeDtypeStruct(s_out_all.shape, F32)],
        scratch_shapes=[pltpu.VMEM((group, GLA_HEADS, GLA_DK, GLA_DV), F32)]
        + [pltpu.VMEM((chunk, GLA_DK), F32)] * (group * GLA_HEADS),
        input_output_aliases={7: 1},
        compiler_params=_params("parallel", "arbitrary"),
        name="gla",
    )(q, k, v, g, la, onorm_g, s0_all, s_out_all)


def _merge_body(x_ref, ng_ref, wg_ref, oa_ref, sbg_ref, ob, oc_ref, wa_ref, wb_ref, wc_ref, wo_ref, y_ref):
    x = x_ref[...]
    ms = jnp.mean(x * x, axis=-1, keepdims=True)
    h = (x * lax.rsqrt(ms + EPS) * ng_ref[...]).astype(BF16)
    a = (oa_ref[...].astype(F32) * _silu(sbg_ref[...].astype(F32))).astype(BF16)
    branches = (_dot(a, wa_ref[...]), _dot(ob, wb_ref[...]), _dot(oc_ref[...], wc_ref[...]))
    merged = None
    for n, p in enumerate(branches):
        gate = 1.0 / (1.0 + jnp.exp(-_dot(h, wg_ref[:, n * D_MODEL:(n + 1) * D_MODEL])))
        merged = gate * p if merged is None else merged + gate * p
    y_ref[...] = x + _dot(merged.astype(BF16), wo_ref[...])


def _merge_kernel(x_ref, ng_ref, wg_ref, oa_ref, sbg_ref, ob_ref, oc_ref, wa_ref, wb_ref, wc_ref, wo_ref, y_ref):
    _merge_body(x_ref, ng_ref, wg_ref, oa_ref, sbg_ref, ob_ref[...], oc_ref, wa_ref, wb_ref, wc_ref, wo_ref, y_ref)


def _merge_pool_kernel(x_ref, ng_ref, wg_ref, oa_ref, sbg_ref, u_ref, halo_ref, plg_ref, pw_ref, scale_ref, oc_ref,
                       wa_ref, wb_ref, wc_ref, wo_ref, y_ref, *, tm, seq):
    ob = _pool_prompt_block(u_ref, halo_ref, plg_ref, pw_ref, scale_ref, tm=tm, seq=seq)
    _merge_body(x_ref, ng_ref, wg_ref, oa_ref, sbg_ref, ob, oc_ref, wa_ref, wb_ref, wc_ref, wo_ref, y_ref)


def _merge(x, oa, sbg, pool, oc, lw, tm, seq=None):
    m = x.shape[0]
    assert m % tm == 0
    row = lambda w: pl.BlockSpec((tm, w), lambda i: (i, 0))
    full = lambda a: pl.BlockSpec(a.shape, lambda i: (0,) * a.ndim)
    if seq is None:
        body, pool_specs, pool_args = _merge_kernel, [row(POOL_WIDTH)], [pool]
    else:
        assert seq % tm == 0 and tm % POOL_HALO == 0
        ratio = tm // POOL_HALO
        body = functools.partial(_merge_pool_kernel, tm=tm, seq=seq)
        pool_specs = [row(POOL_WIDTH),
                      pl.BlockSpec((POOL_HALO, POOL_WIDTH), lambda i: (jnp.maximum(i * ratio - 1, 0), 0)),
                      row(POOL_WIDTH), full(lw["pool_w"]), full(lw["pool_scale"])]
        pool_args = [pool[0], pool[0], pool[1], lw["pool_w"], lw["pool_scale"]]
    return pl.pallas_call(
        body,
        grid=(m // tm,),
        in_specs=[row(D_MODEL), full(lw["norm_g"]), full(lw["w_mg"]), row(SB_WIDTH), row(SB_WIDTH)] + pool_specs
        + [row(GLA_VAL_WIDTH), full(lw["w_pa"]), full(lw["w_pb"]), full(lw["w_pc"]), full(lw["w_o"])],
        out_specs=row(D_MODEL),
        out_shape=jax.ShapeDtypeStruct((m, D_MODEL), F32),
        compiler_params=_params("parallel"),
        name="merge",
    )(x, lw["norm_g"], lw["w_mg"], oa, sbg, *pool_args, oc, lw["w_pa"], lw["w_pb"], lw["w_pc"], lw["w_o"])


def _layer_weights(l, norm_g, w_in, sb_qnorm_g, sb_knorm_g, pool_w, pool_scale, gla_w2, gla_b2, gla_onorm_g,
                   w_pa, w_pb, w_pc, w_o):
    w = w_in[l]
    head = jnp.arange(SB_WIDTH, dtype=jnp.int32) // SB_HEAD_DIM
    return {
        "norm_g": norm_g[l][None, :],
        "w_main": w[:, :N_MAIN].astype(BF16),
        "w_r": jnp.pad(w[:, N_MAIN:N_MAIN + GLA_RANK], ((0, 0), (0, LANES - GLA_RANK))).astype(BF16),
        "w_mg": w[:, N_MAIN + GLA_RANK:].astype(BF16),
        "w2": jnp.pad(gla_w2[l], ((0, LANES - GLA_RANK), (0, 0))).astype(BF16),
        "b2": gla_b2[l][None, :],
        "qg": jnp.tile(sb_qnorm_g[l], SB_HEADS)[None, :],
        "kg": jnp.tile(sb_knorm_g[l], SB_HEADS)[None, :],
        "head_mean": jnp.where(head[:, None] == head[None, :], 1.0 / SB_HEAD_DIM, 0.0).astype(BF16),
        "pool_w": pool_w[l].astype(BF16),
        "pool_scale": pool_scale[l][None, :],
        "onorm_g": gla_onorm_g[l][None, :],
        "w_pa": w_pa[l].astype(BF16),
        "w_pb": w_pb[l].astype(BF16),
        "w_pc": w_pc[l].astype(BF16),
        "w_o": w_o[l].astype(BF16),
    }


def _new_outputs(depth, n_seq, seq, batch):
    kv = (depth, n_seq, SB_WIDTH, seq)
    return jnp.zeros(kv, F32), jnp.zeros(kv, F32), jnp.zeros((depth, batch, GLA_HEADS, GLA_DK, GLA_DV), F32)


def _layer_prompt(x, lw, sb_bias_l, batch, seq, layer, carried, tm=256, gla_chunk=128):
    p = _proj(x, lw, tm, seq, layer, carried[:2])
    oa = _attn_prompt(p["q"], p["kb"], p["vb"], sb_bias_l * LOG2E, batch, seq)
    s0 = jnp.zeros((1, batch, GLA_HEADS, GLA_DK, GLA_DV), F32)
    oc, s_out = _gla(p["glq"], p["glk"], p["glv"], p["glg"], p["la"], lw["onorm_g"], s0, 0, carried[2], layer,
                     gla_chunk)
    y = _merge(x, oa, p["sbg"], (p["plu"], p["plg"]), oc, lw, tm, seq)
    pool_state = p["plu"].reshape(batch, seq, POOL_WIDTH)[:, seq - POOL_STATE:]
    return y, (p["kt"], p["vt"], s_out), pool_state


def _layer_sample(x, lw, sb_bias_l, cache_k, cache_v, page_table, pool_prev, gla_all, base, batch, t_new, layer,
                  carried, tm, gla_group=4):
    t_pad = SUBLANES
    page = cache_k.shape[2]
    assert t_new <= t_pad and page_table.shape[0] // batch * page >= POOL_STATE

    def pad_t(a):
        a = a.reshape(batch, t_new, a.shape[-1])
        return jnp.pad(a, ((0, 0), (0, t_pad - t_new), (0, 0))).reshape(batch * t_pad, a.shape[-1])

    def unpad_t(a):
        return a.reshape(batch, t_pad, a.shape[-1])[:, :t_new].reshape(batch * t_new, a.shape[-1])

    p = _proj(x, lw, tm, batch * t_new, layer, carried[:2])
    bias_rows = jnp.broadcast_to(jnp.repeat(sb_bias_l * LOG2E, t_pad)[:, None], (SB_HEADS * t_pad, page))
    oa = unpad_t(_attn_sample(pad_t(p["q"]), pad_t(p["kb"]), pad_t(p["vb"]), cache_k, cache_v, page_table,
                              bias_rows, base, t_pad))

    u_new = p["plu"].reshape(batch, t_new, POOL_WIDTH)
    u_ext = jnp.concatenate([pool_prev, u_new], axis=1)
    gate_t = p["plg"].reshape(batch, t_new, POOL_WIDTH).transpose(1, 0, 2)
    ob = _pool_sample(u_ext.transpose(1, 0, 2), gate_t, lw["pool_w"], lw["pool_scale"])
    ob = ob.transpose(1, 0, 2).reshape(batch * t_new, POOL_WIDTH)

    group = gla_group if batch % gla_group == 0 else 1
    oc, s_out = _gla(pad_t(p["glq"]), pad_t(p["glk"]), pad_t(p["glv"]), pad_t(p["glg"]), pad_t(p["la"]),
                     lw["onorm_g"], gla_all, layer, carried[2], layer, t_pad, group)
    y = _merge(x, oa, p["sbg"], ob, unpad_t(oc), lw, tm)
    return y, (p["kt"], p["vt"], s_out), u_ext[:, t_new:]


def kernel(x_prompt, x_sample, cache_k, cache_v, state_pool, state_gla, page_table, norm_g, w_in, sb_qnorm_g,
           sb_knorm_g, sb_bias, pool_w, pool_scale, gla_w2, gla_b2, gla_onorm_g, w_pa, w_pb, w_pc, w_o):
    bp, seq, _ = x_prompt.shape
    bs, t_new, _ = x_sample.shape
    depth, n_pool, page = cache_k.shape[:3]
    ck = cache_k.transpose(0, 1, 3, 4, 2).reshape(depth * n_pool, SB_WIDTH, page)
    cv = cache_v.transpose(0, 1, 3, 4, 2).reshape(depth * n_pool, SB_WIDTH, page)
    pt = page_table.reshape(-1).astype(jnp.int32)
    y_p = x_prompt.reshape(bp * seq, D_MODEL)
    y_s = x_sample.reshape(bs * t_new, D_MODEL)
    car_p = _new_outputs(depth, bp, seq, bp)
    car_s = _new_outputs(depth, 1, bs * t_new, bs)
    pools_p, pools_s = [], []
    for l in range(depth):
        lw = _layer_weights(l, norm_g, w_in, sb_qnorm_g, sb_knorm_g, pool_w, pool_scale, gla_w2, gla_b2,
                            gla_onorm_g, w_pa, w_pb, w_pc, w_o)
        y_p, car_p, pool_p = _layer_prompt(y_p, lw, sb_bias[l], bp, seq, l, car_p)
        y_s, car_s, pool_s = _layer_sample(y_s, lw, sb_bias[l], ck, cv, pt, state_pool[l], state_gla, l * n_pool, bs,
                                           t_new, l, car_s, tm=min(256, bs * t_new))
        pools_p.append(pool_p)
        pools_s.append(pool_s)

    def token_major(kt, nb, t):
        return kt.reshape(depth, SB_HEADS, SB_HEAD_DIM, nb, t).transpose(0, 3, 4, 1, 2)

    return (y_p.reshape(bp, seq, D_MODEL), y_s.reshape(bs, t_new, D_MODEL),
            token_major(car_p[0].transpose(0, 2, 1, 3), bp, seq), token_major(car_p[1].transpose(0, 2, 1, 3), bp, seq),
            jnp.stack(pools_p), car_p[2],
            token_major(car_s[0], bs, t_new), token_major(car_s[1], bs, t_new), jnp.stack(pools_s), car_s[2])
```

```python
import functools
import math

import jax
import jax.numpy as jnp
from jax import lax
from jax.experimental import pallas as pl
from jax.experimental.pallas import tpu as pltpu

F32 = jnp.float32
BF16 = jnp.bfloat16

D_MODEL = 1024
SB_HEADS = 8
SB_HEAD_DIM = 64
SB_WIDTH = SB_HEADS * SB_HEAD_DIM
POOL_WINDOWS = (2, 4, 8, 16)
POOL_GROUPS = 4
POOL_WIDTH = 512
POOL_GROUP_DIM = POOL_WIDTH // POOL_GROUPS
POOL_STATE = max(POOL_WINDOWS) - 1
POOL_HALO = 16
GLA_HEADS = 4
GLA_DK = 128
GLA_DV = 256
GLA_KEY_WIDTH = GLA_HEADS * GLA_DK
GLA_VAL_WIDTH = GLA_HEADS * GLA_DV
GLA_RANK = 16
GLA_TAU = 16.0
N_BRANCH = 3
EPS = 1e-6
LOG2E = math.log2(math.e)
LANES = 128
SUBLANES = 8
N_MAIN = 4 * SB_WIDTH + 2 * POOL_WIDTH + 2 * GLA_KEY_WIDTH + 2 * GLA_VAL_WIDTH
VMEM_LIMIT = 48 * 1024 * 1024

_NT = (((1,), (1,)), ((), ()))
_TN = (((0,), (0,)), ((), ()))


def _dot(a, b):
    return jnp.dot(a, b, preferred_element_type=F32)


def _dot_nt(a, b):
    return lax.dot_general(a, b, _NT, preferred_element_type=F32)


def _silu(g):
    return g / (1.0 + jnp.exp(-g))


def _softplus2(z):
    one = jnp.ones((), z.dtype)
    return jnp.maximum(z, jnp.zeros((), z.dtype)) + jnp.log2(one + jnp.exp2(-jnp.abs(z)))


def _split_bf16(x):
    hi = x.astype(BF16)
    lo = (x - hi.astype(F32)).astype(BF16)
    return hi, lo


def _params(*sem):
    return pltpu.CompilerParams(dimension_semantics=sem, vmem_limit_bytes=VMEM_LIMIT)


def _proj_kernel(x_ref, ng_ref, w_ref, wr_ref, w2_ref, b2_ref, qg_ref, kg_ref, hm_ref, kt_all_ref, vt_all_ref,
                 q_ref, kt_ref, vt_ref, kb_ref, vb_ref, sbg_ref, plu_ref, plg_ref,
                 glq_ref, glk_ref, glv_ref, glg_ref, la_ref):
    del kt_all_ref, vt_all_ref
    x = x_ref[...]
    ms = jnp.mean(x * x, axis=-1, keepdims=True)
    h = (x * lax.rsqrt(ms + EPS) * ng_ref[...]).astype(BF16)

    def cols(start, width):
        return _dot(h, w_ref[:, start:start + width])

    def head_norm(y, g):
        y2 = (y * y).astype(BF16)
        half = SB_WIDTH // 2
        m = jnp.concatenate([_dot(y2[:, :half], hm_ref[:half, :half]), _dot(y2[:, half:], hm_ref[half:, half:])],
                            axis=1)
        return y * lax.rsqrt(m + EPS) * g

    q_raw = cols(0, SB_WIDTH)
    k_raw = cols(SB_WIDTH, SB_WIDTH)
    r = _dot(h, wr_ref[...])
    v = cols(2 * SB_WIDTH, SB_WIDTH)
    vt_ref[...] = v.T
    vb_ref[...] = v.astype(BF16)
    sbg_ref[...] = cols(3 * SB_WIDTH, SB_WIDTH).astype(BF16)
    off = 4 * SB_WIDTH
    plu_ref[...] = cols(off, POOL_WIDTH)
    plg_ref[...] = cols(off + POOL_WIDTH, POOL_WIDTH).astype(BF16)
    off += 2 * POOL_WIDTH
    glq_ref[...] = cols(off, GLA_KEY_WIDTH).astype(BF16)
    glk_ref[...] = cols(off + GLA_KEY_WIDTH, GLA_KEY_WIDTH).astype(BF16)
    off += 2 * GLA_KEY_WIDTH
    glv_ref[...] = cols(off, GLA_VAL_WIDTH).astype(BF16)
    glg_ref[...] = cols(off + GLA_VAL_WIDTH, GLA_VAL_WIDTH).astype(BF16)
    qn = head_norm(q_raw, qg_ref[...])
    q_ref[...] = (qn * (LOG2E * SB_HEAD_DIM ** -0.5)).astype(BF16)
    kn = head_norm(k_raw, kg_ref[...])
    kt_ref[...] = kn.T
    kb_ref[...] = kn.astype(BF16)
    pre = _dot(r.astype(BF16), w2_ref[...]) + b2_ref[...]
    la_ref[...] = (jnp.minimum(pre, 0.0) - jnp.log(1.0 + jnp.exp(-jnp.abs(pre)))) * (1.0 / GLA_TAU)


def _proj(x, lw, tm, seq, layer, kv_all):
    m = x.shape[0]
    assert m % seq == 0 and seq % tm == 0
    per_seq = seq // tm
    row = lambda w: pl.BlockSpec((tm, w), lambda i: (i, 0))
    full = lambda a: pl.BlockSpec(a.shape, lambda i: (0,) * a.ndim)
    feat = pl.BlockSpec((None, None, SB_WIDTH, tm), lambda i: (layer, i // per_seq, 0, i % per_seq))
    consts = (lw["norm_g"], lw["w_main"], lw["w_r"], lw["w2"], lw["b2"], lw["qg"], lw["kg"], lw["head_mean"])
    widths = (SB_WIDTH, None, None, SB_WIDTH, SB_WIDTH, SB_WIDTH, POOL_WIDTH, POOL_WIDTH,
              GLA_KEY_WIDTH, GLA_KEY_WIDTH, GLA_VAL_WIDTH, GLA_VAL_WIDTH, GLA_KEY_WIDTH)
    dtypes = (BF16, F32, F32, BF16, BF16, BF16, F32, BF16, BF16, BF16, BF16, BF16, F32)
    kv_shape = jax.ShapeDtypeStruct(kv_all[0].shape, F32)
    assert kv_all[0].shape[1:] == (m // seq, SB_WIDTH, seq)
    n_in = 1 + len(consts)
    outs = pl.pallas_call(
        _proj_kernel,
        grid=(m // tm,),
        in_specs=[row(D_MODEL)] + [full(a) for a in consts] + [pl.BlockSpec(memory_space=pl.ANY)] * 2,
        out_specs=[feat if w is None else row(w) for w in widths],
        out_shape=[kv_shape if w is None else jax.ShapeDtypeStruct((m, w), d) for w, d in zip(widths, dtypes)],
        input_output_aliases={n_in: 1, n_in + 1: 2},
        compiler_params=_params("parallel"),
        name="proj",
    )(x, *consts, *kv_all)
    names = ("q", "kt", "vt", "kb", "vb", "sbg", "plu", "plg", "glq", "glk", "glv", "glg", "la")
    return dict(zip(names, outs))


def _attn_prompt_kernel(bias_ref, q_ref, k_ref, v_ref, o_ref, *, tq, tk, unrolls, heads):
    g = pl.program_id(1)
    i = pl.program_id(2)
    lane = lax.broadcasted_iota(jnp.int32, (tq, LANES), 1)
    tri = jnp.where(lax.broadcasted_iota(jnp.int32, (tk, tk), 0) > lax.broadcasted_iota(jnp.int32, (tk, tk), 1),
                    1.0, 0.0).astype(BF16)
    kd = (i * tq) // tk
    qhs, biases = [], []
    for hh in range(heads):
        q = q_ref[:, (hh // 2) * LANES:(hh // 2 + 1) * LANES]
        qhs.append(jnp.where((lane >= SB_HEAD_DIM) == (hh % 2 == 1), q, jnp.zeros_like(q)))
        biases.append(bias_ref[heads * g + hh])

    def front(hh, kb, valid):
        ks = k_ref[pl.ds(pl.multiple_of(kb * tk, tk), tk), (hh // 2) * LANES:(hh // 2 + 1) * LANES]
        z = _dot_nt(qhs[hh], ks) + biases[hh]
        sp = _softplus2(z)
        if valid is not None:
            sp = jnp.where(valid, sp, 0.0)
        later = _dot(sp.astype(BF16), tri)
        return (z - sp) - later, later[:, 0:1] + sp[:, 0:1]

    def back(hh, pre, tot, kb, valid, acc, c):
        vs = v_ref[pl.ds(pl.multiple_of(kb * tk, tk), tk), (hh // 2) * LANES:(hh // 2 + 1) * LANES]
        a = jnp.exp2(pre - c)
        if valid is not None:
            a = jnp.where(valid, a, 0.0)
        return acc + _dot(a.astype(BF16), vs), c + tot

    def blocks(kbs, valid, carry):
        fronts = [[front(hh, kb, valid) for hh in range(heads)] for kb in kbs]
        carry = list(carry)
        for kb, fr in zip(kbs, fronts):
            for hh in range(heads):
                carry[hh] = back(hh, *fr[hh], kb, valid, *carry[hh])
        return tuple(carry)

    q_pos = i * tq + lax.broadcasted_iota(jnp.int32, (tq, tk), 0)
    k_pos = kd * tk + lax.broadcasted_iota(jnp.int32, (tq, tk), 1)
    zero = (jnp.zeros((tq, LANES), F32), jnp.zeros((tq, 1), F32))
    carry = blocks([kd], k_pos < q_pos, (zero,) * heads)
    left = kd
    for u in unrolls:
        top = left
        carry = lax.fori_loop(
            0, left // u, lambda j, cr: blocks([top - 1 - j * u - r for r in range(u)], None, cr), carry)
        left = left % u
    for t in range(heads // 2):
        o_ref[:, t * LANES:(t + 1) * LANES] = jnp.where(
            lane < SB_HEAD_DIM, carry[2 * t][0], carry[2 * t + 1][0]).astype(o_ref.dtype)


def _attn_prompt(q, k, v, bias, batch, seq, tq=256, tk=256, unrolls=(4, 1), heads=4):
    assert seq % tk == 0 and tk % tq == 0 and unrolls[-1] == 1 and heads % 2 == 0 and SB_HEADS % heads == 0
    nq = seq // tq
    width = heads // 2 * LANES
    return pl.pallas_call(
        functools.partial(_attn_prompt_kernel, tq=tq, tk=tk, unrolls=unrolls, heads=heads),
        grid=(batch, SB_HEADS // heads, nq),
        in_specs=[
            pl.BlockSpec(memory_space=pltpu.SMEM),
            pl.BlockSpec((tq, width), lambda b, g, i: (b * nq + i, g)),
            pl.BlockSpec((seq, width), lambda b, g, i: (b, g)),
            pl.BlockSpec((seq, width), lambda b, g, i: (b, g)),
        ],
        out_specs=pl.BlockSpec((tq, width), lambda b, g, i: (b * nq + i, g)),
        out_shape=jax.ShapeDtypeStruct((batch * seq, SB_WIDTH), BF16),
        compiler_params=_params("parallel", "parallel", "arbitrary"),
        name="attn_prompt",
    )(bias, q, k, v)


def _attn_sample_kernel(pt_ref, bias_ref, q_ref, kn_ref, vn_ref, kc_hbm, vc_hbm, o_ref, kbuf, vbuf, sems,
                        *, page, t_pad, n_pages, base, kblk):
    b = pl.program_id(0)
    nb = pl.num_programs(0)
    slot = b % 2

    def page_copies(seq, slot_):
        copies = []
        for p in range(n_pages):
            idx = base + pt_ref[seq * n_pages + p]
            dst = pl.ds(p * page, page)
            copies.append(pltpu.make_async_copy(kc_hbm.at[idx], kbuf.at[slot_, :, dst], sems.at[slot_, 0, p]))
            copies.append(pltpu.make_async_copy(vc_hbm.at[idx], vbuf.at[slot_, :, dst], sems.at[slot_, 1, p]))
        return copies

    @pl.when(b == 0)
    def _():
        for cp in page_copies(0, 0):
            cp.start()

    @pl.when(b + 1 < nb)
    def _():
        for cp in page_copies(b + 1, 1 - slot):
            cp.start()

    rows = SB_HEADS * t_pad
    row = lax.broadcasted_iota(jnp.int32, (rows, SB_WIDTH), 0)
    lane = lax.broadcasted_iota(jnp.int32, (rows, SB_WIDTH), 1)
    own = (lane // SB_HEAD_DIM) == (row // t_pad)
    q8 = q_ref[...]
    qbd = jnp.where(own, jnp.concatenate([q8] * SB_HEADS, axis=0), jnp.zeros((rows, SB_WIDTH), q8.dtype))

    def strict_tri(n):
        return jnp.where(lax.broadcasted_iota(jnp.int32, (n, n), 0) > lax.broadcasted_iota(jnp.int32, (n, n), 1),
                         1.0, 0.0).astype(BF16)

    pad = jnp.zeros((page - t_pad, SB_WIDTH), BF16)
    key_idx = lax.broadcasted_iota(jnp.int32, (rows, page), 1)
    qry_idx = lax.broadcasted_iota(jnp.int32, (rows, page), 0) % t_pad
    valid = key_idx < qry_idx
    z = _dot_nt(qbd, jnp.concatenate([kn_ref[...], pad], axis=0)) + bias_ref[...]
    sp = jnp.where(valid, _softplus2(z), 0.0)
    later = _dot(sp.astype(BF16), strict_tri(page))
    a = jnp.where(valid, jnp.exp2((z - sp) - later), 0.0)
    acc = _dot(a.astype(BF16), jnp.concatenate([vn_ref[...], pad], axis=0))
    c = later[:, 0:1] + sp[:, 0:1]

    for cp in page_copies(b, slot):
        cp.wait()

    past = n_pages * page
    z = _dot(qbd, kbuf[slot].astype(BF16)) + bias_ref[:, 0:1]
    sp = _softplus2(z)
    tri = strict_tri(kblk)
    a_blocks = [None] * (past // kblk)
    for j in reversed(range(past // kblk)):
        cols = slice(j * kblk, (j + 1) * kblk)
        later = _dot(sp[:, cols].astype(BF16), tri)
        a_blocks[j] = jnp.exp2((z[:, cols] - sp[:, cols]) - later - c).astype(BF16)
        c = c + (later[:, 0:1] + sp[:, j * kblk:j * kblk + 1])
    acc = acc + _dot_nt(jnp.concatenate(a_blocks, axis=1), vbuf[slot].astype(BF16))

    picked = jnp.where(own, acc, 0.0)
    o_ref[...] = picked.reshape(SB_HEADS, t_pad, SB_WIDTH).sum(axis=0).astype(o_ref.dtype)


def _attn_sample(q, kn, vn, cache_kt, cache_vt, page_table, bias_rows, base, t_pad, kblk=256):
    batch = q.shape[0] // t_pad
    n_pages = page_table.shape[0] // batch
    page = cache_kt.shape[2]
    kblk = min(kblk, n_pages * page)
    assert (n_pages * page) % kblk == 0
    new_spec = pl.BlockSpec((t_pad, SB_WIDTH), lambda b, pt: (b, 0))
    hbm = pl.BlockSpec(memory_space=pl.ANY)
    grid_spec = pltpu.PrefetchScalarGridSpec(
        num_scalar_prefetch=1,
        grid=(batch,),
        in_specs=[pl.BlockSpec(bias_rows.shape, lambda b, pt: (0, 0)), new_spec, new_spec, new_spec, hbm, hbm],
        out_specs=new_spec,
        scratch_shapes=[pltpu.VMEM((2, SB_WIDTH, n_pages * page), F32), pltpu.VMEM((2, SB_WIDTH, n_pages * page), F32),
                        pltpu.SemaphoreType.DMA((2, 2, n_pages))],
    )
    return pl.pallas_call(
        functools.partial(_attn_sample_kernel, page=page, t_pad=t_pad, n_pages=n_pages, base=base, kblk=kblk),
        grid_spec=grid_spec,
        out_shape=jax.ShapeDtypeStruct(q.shape, BF16),
        compiler_params=_params("arbitrary"),
        name="attn_sample",
    )(page_table, bias_rows, q, kn, vn, cache_kt, cache_vt)


def _pool_mix(pooled_groups, pw_ref, scale_ref, gate):
    mixed = jnp.concatenate([_dot(p.astype(BF16), pw_ref[g]) for g, p in enumerate(pooled_groups)], axis=1)
    return mixed * scale_ref[...] * _silu(gate)


def _pool_prompt_block(u_ref, halo_ref, g_ref, pw_ref, scale_ref, *, tm, seq):
    i = pl.program_id(0)
    start = (i * tm) % seq
    u = u_ref[...]
    halo = jnp.where(start == 0, 0.0, halo_ref[...])
    ext = jnp.concatenate([halo, u], axis=0)
    pos = start + lax.broadcasted_iota(jnp.int32, (tm, 1), 0)
    pooled = []
    for g, w in enumerate(POOL_WINDOWS):
        s = ext[:, g * POOL_GROUP_DIM:(g + 1) * POOL_GROUP_DIM]
        k = 1
        while k < w:
            s = s + pltpu.roll(s, k, 0)
            k *= 2
        count = jnp.minimum(w, pos + 1).astype(F32)
        pooled.append(s[POOL_HALO:] / count - u[:, g * POOL_GROUP_DIM:(g + 1) * POOL_GROUP_DIM])
    return _pool_mix(pooled, pw_ref, scale_ref, g_ref[...].astype(F32)).astype(BF16)


def _pool_sample_kernel(ue_ref, g_ref, pw_ref, scale_ref, o_ref, *, t_new):
    batch = ue_ref.shape[1]
    pooled = []
    for g, w in enumerate(POOL_WINDOWS):
        lanes = slice(g * POOL_GROUP_DIM, (g + 1) * POOL_GROUP_DIM)
        per_t = []
        for t in range(t_new):
            cur = POOL_STATE + t
            s = ue_ref[cur - w + 1, :, lanes]
            for r in range(cur - w + 2, cur + 1):
                s = s + ue_ref[r, :, lanes]
            per_t.append(s / float(w) - ue_ref[cur, :, lanes])
        pooled.append(jnp.concatenate(per_t, axis=0))
    gate = g_ref[...].reshape(t_new * batch, POOL_WIDTH).astype(F32)
    o_ref[...] = _pool_mix(pooled, pw_ref, scale_ref, gate).reshape(t_new, batch, POOL_WIDTH).astype(o_ref.dtype)


def _pool_sample(u_ext_t, gate_t, pool_w, pool_scale):
    t_new = gate_t.shape[0]
    return pl.pallas_call(
        functools.partial(_pool_sample_kernel, t_new=t_new),
        out_shape=jax.ShapeDtypeStruct(gate_t.shape, BF16),
        compiler_params=pltpu.CompilerParams(vmem_limit_bytes=VMEM_LIMIT),
        name="pool_sample",
    )(u_ext_t, gate_t, pool_w, pool_scale)


def _block_reference_rows(b, bs_ref, m, t_idx):
    c = b.shape[0]
    if 2 * m >= SUBLANES:
        parts = [jnp.broadcast_to(bs_ref[p * 2 * m + m - 1:p * 2 * m + m, :], (2 * m, b.shape[1]))
                 for p in range(c // (2 * m))]
        return parts[0] if len(parts) == 1 else jnp.concatenate(parts, axis=0)
    if m == 2:
        t4 = t_idx & 3
        return jnp.where(t4 == 0, pltpu.roll(b, c - 1, 0),
                         jnp.where(t4 == 1, b, jnp.where(t4 == 2, pltpu.roll(b, 1, 0), pltpu.roll(b, 2, 0))))
    assert m == 1
    return jnp.where((t_idx & 1) == 0, b, pltpu.roll(b, 1, 0))


def _gla_kernel(q_ref, k_ref, v_ref, g_ref, la_ref, og_ref, s0_ref, sout_all_ref, o_ref, sout_ref, s_ref, *bs_refs,
                chunk, group):
    del sout_all_ref
    ci = pl.program_id(1)

    @pl.when(ci == 0)
    def _():
        s_ref[...] = s0_ref[...]

    t_row = lax.broadcasted_iota(jnp.int32, (chunk, chunk), 0)
    t_col = lax.broadcasted_iota(jnp.int32, (chunk, chunk), 1)
    t_idx = lax.broadcasted_iota(jnp.int32, (chunk, GLA_DK), 0)
    tril = jnp.where(t_row >= t_col, 1.0, 0.0).astype(BF16)
    below = jnp.where(t_row > t_col, t_row ^ t_col, 0)
    level_mask = {}
    m = chunk // 2
    while m >= 1:
        level_mask[m] = (below - m).astype(jnp.uint32) < jnp.uint32(m)
        m //= 2

    def one_head(gi, h, s):
        rows = slice(gi * chunk, (gi + 1) * chunk)
        kl = slice(h * GLA_DK, (h + 1) * GLA_DK)
        vl = slice(h * GLA_DV, (h + 1) * GLA_DV)
        bs = bs_refs[gi * GLA_HEADS + h]
        hi, lo = _split_bf16(la_ref[rows, kl])
        b = _dot(tril, hi) + _dot(tril, lo)
        yield
        bs[...] = b
        q = q_ref[rows, kl].astype(F32) * (GLA_DK ** -0.5)
        k = k_ref[rows, kl].astype(F32)
        v = v_ref[rows, vl]

        o_inter = _dot((q * jnp.exp(b)).astype(BF16), s.astype(BF16))
        terms = [(t_row == t_col, _dot_nt(q.astype(BF16), k.astype(BF16)))]
        m = chunk // 2
        while m >= 1:
            e = jnp.exp(-jnp.abs(b - _block_reference_rows(b, bs, m, t_idx)))
            mixed = (jnp.where((t_idx & m) != 0, q, k) * e).astype(BF16)
            terms.append((level_mask[m], _dot_nt(mixed, mixed)))
            m //= 2
        b_end = b[chunk - 1:chunk, :]
        k_end = (k * jnp.exp(b_end - b)).astype(BF16)
        upd = lax.dot_general(k_end, v, _TN, preferred_element_type=F32)
        yield
        scores = None
        for keep, term in terms:
            part = jnp.where(keep, term, 0.0)
            scores = part if scores is None else scores + part
        o_intra = _dot(scores.astype(BF16), v)
        decay_col = jnp.broadcast_to(jnp.exp(b_end), (GLA_DK, GLA_DK)).T
        s_new = s * jnp.concatenate([decay_col] * (GLA_DV // GLA_DK), axis=1) + upd
        yield
        o = o_inter + o_intra
        on = o * lax.rsqrt(jnp.mean(o * o, axis=-1, keepdims=True) + EPS) * og_ref[...]
        return s_new, (on * _silu(g_ref[rows, vl].astype(F32))).astype(o_ref.dtype)

    pairs = [(gi, h) for gi in range(group) for h in range(GLA_HEADS)]
    chains = [one_head(gi, h, s_ref[gi, h]) for gi, h in pairs]
    results = [None] * len(chains)
    while any(r is None for r in results):
        for n, chain in enumerate(chains):
            if results[n] is None:
                try:
                    next(chain)
                except StopIteration as done:
                    results[n] = done.value
    for (gi, h), (s_new, out) in zip(pairs, results):
        s_ref[gi, h] = s_new
        o_ref[gi * chunk:(gi + 1) * chunk, h * GLA_DV:(h + 1) * GLA_DV] = out

    @pl.when(ci == pl.num_programs(1) - 1)
    def _():
        sout_ref[...] = s_ref[...]


def _gla(q, k, v, g, la, onorm_g, s0_all, s0_layer, s_out_all, layer, chunk, group=1):
    batch = s0_all.shape[1]
    nc = q.shape[0] // (batch * chunk)
    assert batch % group == 0 and (group == 1 or nc == 1)
    tok = lambda w: pl.BlockSpec((group * chunk, w), lambda b, c: (b * nc + c, 0))
    state = lambda row: pl.BlockSpec((None, group, GLA_HEADS, GLA_DK, GLA_DV), lambda b, c: (row, b, 0, 0, 0))
    assert s_out_all.shape[1:] == s0_all.shape[1:]
    return pl.pallas_call(
        functools.partial(_gla_kernel, chunk=chunk, group=group),
        grid=(batch // group, nc),
        in_specs=[tok(GLA_KEY_WIDTH), tok(GLA_KEY_WIDTH), tok(GLA_VAL_WIDTH), tok(GLA_VAL_WIDTH), tok(GLA_KEY_WIDTH),
                  pl.BlockSpec(onorm_g.shape, lambda b, c: (0, 0)), state(s0_layer), pl.BlockSpec(memory_space=pl.ANY)],
        out_specs=[tok(GLA_VAL_WIDTH), state(layer)],
        out_shape=[jax.ShapeDtypeStruct(v.shape, BF16), jax.ShapeDtypeStruct(s_out_all.shape, F32)],
        scratch_shapes=[pltpu.VMEM((group, GLA_HEADS, GLA_DK, GLA_DV), F32)]
        + [pltpu.VMEM((chunk, GLA_DK), F32)] * (group * GLA_HEADS),
        input_output_aliases={7: 1},
        compiler_params=_params("parallel", "arbitrary"),
        name="gla",
    )(q, k, v, g, la, onorm_g, s0_all, s_out_all)


def _merge_body(x_ref, ng_ref, wg_ref, oa_ref, sbg_ref, pool_branch, oc_ref, wa_ref, wb_ref, wc_ref, wo_ref, y_ref):
    x = x_ref[...]
    ms = jnp.mean(x * x, axis=-1, keepdims=True)
    h = (x * lax.rsqrt(ms + EPS) * ng_ref[...]).astype(BF16)
    gates = [_dot(h, wg_ref[:, n * D_MODEL:(n + 1) * D_MODEL]) for n in range(N_BRANCH)]
    a = (oa_ref[...].astype(F32) * _silu(sbg_ref[...].astype(F32))).astype(BF16)
    pa = _dot(a, wa_ref[...])
    pc = _dot(oc_ref[...], wc_ref[...])
    pb = _dot(pool_branch(), wb_ref[...])
    merged = None
    for gate, p in zip(gates, (pa, pb, pc)):
        term = (1.0 / (1.0 + jnp.exp(-gate))) * p
        merged = term if merged is None else merged + term
    y_ref[...] = x + _dot(merged.astype(BF16), wo_ref[...])


def _merge_kernel(x_ref, ng_ref, wg_ref, oa_ref, sbg_ref, ob_ref, oc_ref, wa_ref, wb_ref, wc_ref, wo_ref, y_ref):
    _merge_body(x_ref, ng_ref, wg_ref, oa_ref, sbg_ref, lambda: ob_ref[...], oc_ref, wa_ref, wb_ref, wc_ref, wo_ref,
                y_ref)


def _merge_pool_kernel(x_ref, ng_ref, wg_ref, oa_ref, sbg_ref, u_ref, halo_ref, plg_ref, pw_ref, scale_ref, oc_ref,
                       wa_ref, wb_ref, wc_ref, wo_ref, y_ref, *, tm, seq):
    pool_branch = functools.partial(_pool_prompt_block, u_ref, halo_ref, plg_ref, pw_ref, scale_ref, tm=tm, seq=seq)
    _merge_body(x_ref, ng_ref, wg_ref, oa_ref, sbg_ref, pool_branch, oc_ref, wa_ref, wb_ref, wc_ref, wo_ref, y_ref)


def _merge(x, oa, sbg, pool, oc, lw, tm, seq=None):
    m = x.shape[0]
    assert m % tm == 0
    row = lambda w: pl.BlockSpec((tm, w), lambda i: (i, 0))
    full = lambda a: pl.BlockSpec(a.shape, lambda i: (0,) * a.ndim)
    if seq is None:
        body, pool_specs, pool_args = _merge_kernel, [row(POOL_WIDTH)], [pool]
    else:
        assert seq % tm == 0 and tm % POOL_HALO == 0
        ratio = tm // POOL_HALO
        body = functools.partial(_merge_pool_kernel, tm=tm, seq=seq)
        pool_specs = [row(POOL_WIDTH),
                      pl.BlockSpec((POOL_HALO, POOL_WIDTH), lambda i: (jnp.maximum(i * ratio - 1, 0), 0)),
                      row(POOL_WIDTH), full(lw["pool_w"]), full(lw["pool_scale"])]
        pool_args = [pool[0], pool[0], pool[1], lw["pool_w"], lw["pool_scale"]]
    return pl.pallas_call(
        body,
        grid=(m // tm,),
        in_specs=[row(D_MODEL), full(lw["norm_g"]), full(lw["w_mg"]), row(SB_WIDTH), row(SB_WIDTH)] + pool_specs
        + [row(GLA_VAL_WIDTH), full(lw["w_pa"]), full(lw["w_pb"]), full(lw["w_pc"]), full(lw["w_o"])],
        out_specs=row(D_MODEL),
        out_shape=jax.ShapeDtypeStruct((m, D_MODEL), F32),
        compiler_params=_params("parallel"),
        name="merge",
    )(x, lw["norm_g"], lw["w_mg"], oa, sbg, *pool_args, oc, lw["w_pa"], lw["w_pb"], lw["w_pc"], lw["w_o"])


def _layer_weights(l, norm_g, w_in, sb_qnorm_g, sb_knorm_g, pool_w, pool_scale, gla_w2, gla_b2, gla_onorm_g,
                   w_pa, w_pb, w_pc, w_o):
    w = w_in[l]
    head = jnp.arange(SB_WIDTH, dtype=jnp.int32) // SB_HEAD_DIM
    return {
        "norm_g": norm_g[l][None, :],
        "w_main": w[:, :N_MAIN].astype(BF16),
        "w_r": jnp.pad(w[:, N_MAIN:N_MAIN + GLA_RANK], ((0, 0), (0, LANES - GLA_RANK))).astype(BF16),
        "w_mg": w[:, N_MAIN + GLA_RANK:].astype(BF16),
        "w2": jnp.pad(gla_w2[l], ((0, LANES - GLA_RANK), (0, 0))).astype(BF16),
        "b2": gla_b2[l][None, :],
        "qg": jnp.tile(sb_qnorm_g[l], SB_HEADS)[None, :],
        "kg": jnp.tile(sb_knorm_g[l], SB_HEADS)[None, :],
        "head_mean": jnp.where(head[:, None] == head[None, :], 1.0 / SB_HEAD_DIM, 0.0).astype(BF16),
        "pool_w": pool_w[l].astype(BF16),
        "pool_scale": pool_scale[l][None, :],
        "onorm_g": gla_onorm_g[l][None, :],
        "w_pa": w_pa[l].astype(BF16),
        "w_pb": w_pb[l].astype(BF16),
        "w_pc": w_pc[l].astype(BF16),
        "w_o": w_o[l].astype(BF16),
    }


def _new_outputs(depth, n_seq, seq, batch):
    kv = (depth, n_seq, SB_WIDTH, seq)
    return jnp.zeros(kv, F32), jnp.zeros(kv, F32), jnp.zeros((depth, batch, GLA_HEADS, GLA_DK, GLA_DV), F32)


def _layer_prompt(x, lw, sb_bias_l, batch, seq, layer, carried, tm=256, gla_chunk=128):
    p = _proj(x, lw, tm, seq, layer, carried[:2])
    oa = _attn_prompt(p["q"], p["kb"], p["vb"], sb_bias_l * LOG2E, batch, seq)
    s0 = jnp.zeros((1, batch, GLA_HEADS, GLA_DK, GLA_DV), F32)
    oc, s_out = _gla(p["glq"], p["glk"], p["glv"], p["glg"], p["la"], lw["onorm_g"], s0, 0, carried[2], layer,
                     gla_chunk)
    y = _merge(x, oa, p["sbg"], (p["plu"], p["plg"]), oc, lw, tm, seq)
    pool_state = p["plu"].reshape(batch, seq, POOL_WIDTH)[:, seq - POOL_STATE:]
    return y, (p["kt"], p["vt"], s_out), pool_state


def _layer_sample(x, lw, sb_bias_l, cache_k, cache_v, page_table, pool_prev, gla_all, base, batch, t_new, layer,
                  carried, tm, gla_group=4):
    t_pad = SUBLANES
    page = cache_k.shape[2]
    assert t_new <= t_pad and page_table.shape[0] // batch * page >= POOL_STATE

    def pad_t(a):
        a = a.reshape(batch, t_new, a.shape[-1])
        return jnp.pad(a, ((0, 0), (0, t_pad - t_new), (0, 0))).reshape(batch * t_pad, a.shape[-1])

    def unpad_t(a):
        return a.reshape(batch, t_pad, a.shape[-1])[:, :t_new].reshape(batch * t_new, a.shape[-1])

    p = _proj(x, lw, tm, batch * t_new, layer, carried[:2])
    bias_rows = jnp.broadcast_to(jnp.repeat(sb_bias_l * LOG2E, t_pad)[:, None], (SB_HEADS * t_pad, page))
    oa = unpad_t(_attn_sample(pad_t(p["q"]), pad_t(p["kb"]), pad_t(p["vb"]), cache_k, cache_v, page_table,
                              bias_rows, base, t_pad))

    u_new = p["plu"].reshape(batch, t_new, POOL_WIDTH)
    u_ext = jnp.concatenate([pool_prev, u_new], axis=1)
    gate_t = p["plg"].reshape(batch, t_new, POOL_WIDTH).transpose(1, 0, 2)
    ob = _pool_sample(u_ext.transpose(1, 0, 2), gate_t, lw["pool_w"], lw["pool_scale"])
    ob = ob.transpose(1, 0, 2).reshape(batch * t_new, POOL_WIDTH)

    group = gla_group if batch % gla_group == 0 else 1
    oc, s_out = _gla(pad_t(p["glq"]), pad_t(p["glk"]), pad_t(p["glv"]), pad_t(p["glg"]), pad_t(p["la"]),
                     lw["onorm_g"], gla_all, layer, carried[2], layer, t_pad, group)
    y = _merge(x, oa, p["sbg"], ob, unpad_t(oc), lw, tm)
    return y, (p["kt"], p["vt"], s_out), u_ext[:, t_new:]


def kernel(x_prompt, x_sample, cache_k, cache_v, state_pool, state_gla, page_table, norm_g, w_in, sb_qnorm_g,
           sb_knorm_g, sb_bias, pool_w, pool_scale, gla_w2, gla_b2, gla_onorm_g, w_pa, w_pb, w_pc, w_o):
    bp, seq, _ = x_prompt.shape
    bs, t_new, _ = x_sample.shape
    depth, n_pool, page = cache_k.shape[:3]
    ck = cache_k.transpose(0, 1, 3, 4, 2).reshape(depth * n_pool, SB_WIDTH, page)
    cv = cache_v.transpose(0, 1, 3, 4, 2).reshape(depth * n_pool, SB_WIDTH, page)
    pt = page_table.reshape(-1).astype(jnp.int32)
    y_p = x_prompt.reshape(bp * seq, D_MODEL)
    y_s = x_sample.reshape(bs * t_new, D_MODEL)
    car_p = _new_outputs(depth, bp, seq, bp)
    car_s = _new_outputs(depth, 1, bs * t_new, bs)
    pools_p, pools_s = [], []
    for l in range(depth):
        lw = _layer_weights(l, norm_g, w_in, sb_qnorm_g, sb_knorm_g, pool_w, pool_scale, gla_w2, gla_b2,
                            gla_onorm_g, w_pa, w_pb, w_pc, w_o)
        y_p, car_p, pool_p = _layer_prompt(y_p, lw, sb_bias[l], bp, seq, l, car_p)
        y_s, car_s, pool_s = _layer_sample(y_s, lw, sb_bias[l], ck, cv, pt, state_pool[l], state_gla, l * n_pool, bs,
                                           t_new, l, car_s, tm=min(256, bs * t_new))
        pools_p.append(pool_p)
        pools_s.append(pool_s)

    def token_major(kt, nb, t):
        return kt.reshape(depth, SB_HEADS, SB_HEAD_DIM, nb, t).transpose(0, 3, 4, 1, 2)

    return (y_p.reshape(bp, seq, D_MODEL), y_s.reshape(bs, t_new, D_MODEL),
            token_major(car_p[0].transpose(0, 2, 1, 3), bp, seq), token_major(car_p[1].transpose(0, 2, 1, 3), bp, seq),
            jnp.stack(pools_p), car_p[2],
            token_major(car_s[0], bs, t_new), token_major(car_s[1], bs, t_new), jnp.stack(pools_s), car_s[2])
```
